```python
import math
import jax
import jax.numpy as jnp
from jax import lax
import numpy as np

D_MODEL = 2048
BATCH = 4
SEQ = 2048
DEPTH = 4
DEC_BATCH = 8
DEC_SEQ = 1
PAST_LEN = 16384
PAGE_SIZE = 128

HEAD_DIM = 128
MIX_WIDTH = D_MODEL
H_A = MIX_WIDTH // 2 // HEAD_DIM
H_B = MIX_WIDTH // 2 // HEAD_DIM
W_A = H_A * HEAD_DIM
W_B = H_B * HEAD_DIM
CONV_W = 4
CONV_CH = 3 * W_A
GDN_CHUNK = 64
MOBA_BLOCK = 256
MOBA_TOPK = 3
MOBA_QCHUNK = 16
ROPE_THETA = 10000.0
N_EXPERTS = 32
TOPK_E = 4
D_FF = D_MODEL
SWIGLU_ALPHA = 1.702
SWIGLU_LIMIT = 7.0
MOE_BLOCK = 128
DN_ALPHA = (2 * DEPTH) ** 0.25
DN_BETA = (8 * DEPTH) ** -0.25
LN_EPS = 1e-5
RMS_EPS = 1e-6
IN_COLS = 4 * W_A + 2 * H_A + 3 * W_B
IN_SPLITS = [CONV_CH, CONV_CH + W_A, CONV_CH + W_A + H_A, CONV_CH + W_A + 2 * H_A,
             CONV_CH + W_A + 2 * H_A + W_B, CONV_CH + W_A + 2 * H_A + 2 * W_B]

kernel_name = 'hymba_gdn_moba_moe_step'

F32 = jnp.float32


def layer_norm(x, g, b):
    xf = x.astype(F32)
    mu = jnp.mean(xf, -1, keepdims=True)
    var = jnp.mean(jnp.square(xf - mu), -1, keepdims=True)
    return ((xf - mu) * lax.rsqrt(var + LN_EPS) * g.astype(F32) + b.astype(F32)).astype(x.dtype)


def rms_norm(x, w):
    xf = x.astype(F32)
    return xf * lax.rsqrt(jnp.mean(xf * xf, -1, keepdims=True) + RMS_EPS) * w.astype(F32)


def l2_norm(x):
    xf = x.astype(F32)
    return xf * lax.rsqrt(jnp.sum(xf * xf, -1, keepdims=True) + 1e-6)


def rope(x, pos):
    half = HEAD_DIM // 2
    inv = ROPE_THETA ** (-2.0 * jnp.arange(half, dtype=F32) / HEAD_DIM)
    ang = pos.astype(F32)[:, None] * inv[None, :]
    cos = jnp.cos(ang)[None, :, None, :]
    sin = jnp.sin(ang)[None, :, None, :]
    xf = x.astype(F32)
    x1, x2 = xf[..., :half], xf[..., half:]
    return jnp.concatenate([x1 * cos - x2 * sin, x2 * cos + x1 * sin], -1).astype(x.dtype)


def short_conv(u, prev, w):
    ext = jnp.concatenate([prev.astype(u.dtype), u], axis=1)
    S = u.shape[1]
    out = ext[:, 0:S] * w[0]
    for i in range(1, CONV_W):
        out = out + ext[:, i:i + S] * w[i]
    return jax.nn.silu(out), ext[:, S:]


def gdn_chunked(q, k, v, g, beta, s0):
    B, S, H, _ = q.shape
    Dv = v.shape[-1]
    C = math.gcd(S, GDN_CHUNK)
    nc = S // C

    def chunks(t):
        t = t.astype(F32).reshape(B, nc, C, H, *t.shape[3:])
        return jnp.moveaxis(t, (1, 3), (0, 2))

    tri_incl = jnp.tril(jnp.ones((C, C), bool))
    tri_strict = jnp.tril(jnp.ones((C, C), bool), -1)

    def step(s, inp):
        qc, kc, vc, gc, bc = inp
        gcum = jnp.cumsum(gc, -1)
        diff = gcum[..., :, None] - gcum[..., None, :]
        decay = jnp.where(tri_incl, jnp.exp(jnp.where(tri_incl, diff, 0.0)), 0.0)
        kk = jnp.einsum('bhid,bhjd->bhij', kc, kc)
        lmat = jnp.where(tri_strict, bc[..., :, None] * kk * decay, 0.0)
        rhs = jnp.concatenate([bc[..., None] * vc, (bc * jnp.exp(gcum))[..., None] * kc], -1)
        sol = lax.linalg.triangular_solve(lmat, rhs, left_side=True, lower=True,
                                          transpose_a=False, unit_diagonal=True)
        u = sol[..., :Dv] - jnp.einsum('bhcd,bhde->bhce', sol[..., Dv:], s)
        qk = jnp.einsum('bhid,bhjd->bhij', qc, kc) * decay
        o = (jnp.einsum('bhcd,bhde->bhce', qc * jnp.exp(gcum)[..., None], s)
             + jnp.einsum('bhij,bhje->bhie', qk, u))
        g_last = gcum[..., -1:]
        s_new = (jnp.exp(g_last)[..., None] * s
                 + jnp.einsum('bhcd,bhce->bhde', kc * jnp.exp(g_last - gcum)[..., None], u))
        return s_new, o

    s_fin, o = lax.scan(step, s0.astype(F32), (chunks(q), chunks(k), chunks(v), chunks(g), chunks(beta)))
    o = jnp.moveaxis(o, (0, 2), (1, 3)).reshape(B, S, H, Dv)
    return o, s_fin


def block_view(t, nblk):
    B, T, H, Dh = t.shape
    t = jnp.pad(t, ((0, 0), (0, nblk * MOBA_BLOCK - T), (0, 0), (0, 0)))
    return t.reshape(B, nblk, MOBA_BLOCK, H, Dh).transpose(0, 3, 1, 2, 4)


def block_means(kb):
    return jnp.mean(kb.astype(F32), axis=3).transpose(0, 2, 1, 3)


def gather_blocks(tb, blk):
    bi = jnp.arange(tb.shape[0])[:, None, None, None]
    hi = jnp.arange(tb.shape[1])[None, None, :, None]
    return tb[bi, hi, blk]


def gather_paged_rows(pool, page_table, new, past, blk):
    page = pool.shape[1]
    pos = blk[..., None] * MOBA_BLOCK + jnp.arange(MOBA_BLOCK)
    bi = jnp.arange(new.shape[0])[:, None, None, None, None]
    hi = jnp.arange(new.shape[2])[None, None, :, None, None]
    lp = jnp.clip(pos // page, 0, page_table.shape[1] - 1)
    from_pool = pool[page_table[bi, lp], pos % page, hi].astype(new.dtype)
    from_new = new[bi, jnp.clip(pos - past, 0, new.shape[1] - 1), hi]
    return jnp.where((pos < past)[..., None], from_pool, from_new)


def moba_attend(q, q_pos, kmean, gather_kv):
    B, Sq, H, _ = q.shape
    nblk = kmean.shape[1]
    q_blk = q_pos // MOBA_BLOCK
    s = jnp.einsum('bqhd,bnhd->bqhn', q.astype(F32), kmean)
    n_tot = max(nblk, MOBA_TOPK)
    if nblk < MOBA_TOPK:
        s = jnp.pad(s, ((0, 0), (0, 0), (0, 0), (0, MOBA_TOPK - nblk)))
    fully_past = jnp.arange(n_tot)[None, :] < q_blk[:, None]
    s = jnp.where(fully_past[None, :, None, :], s, -jnp.inf)
    _, sel = lax.top_k(s, MOBA_TOPK)
    sel_ok = sel < q_blk[None, :, None, None]
    own = jnp.broadcast_to(q_blk[None, :, None, None], (B, Sq, H, 1))
    blk = jnp.concatenate([jnp.minimum(sel, nblk - 1), own], -1)
    ok = jnp.concatenate([sel_ok, jnp.ones((B, Sq, H, 1), bool)], -1)
    qc = math.gcd(Sq, MOBA_QCHUNK)
    nch = Sq // qc

    def to_chunks(a):
        return a.reshape(B, nch, qc, *a.shape[2:]).swapaxes(0, 1)

    def one_chunk(args):
        q_c, blk_c, ok_c, pos_c = args
        k_g, v_g = gather_kv(blk_c)
        kpos = blk_c[..., None] * MOBA_BLOCK + jnp.arange(MOBA_BLOCK)
        mask = ok_c[..., None] & (kpos <= pos_c[None, :, None, None, None])
        logits = jnp.einsum('bqhd,bqhskd->bqhsk', q_c, k_g, preferred_element_type=F32)
        logits = jnp.where(mask, logits, -jnp.inf).reshape(B, qc, H, -1)
        p = jax.nn.softmax(logits, -1).reshape(mask.shape).astype(v_g.dtype)
        return jnp.einsum('bqhsk,bqhskd->bqhd', p, v_g)

    out = lax.map(one_chunk, (to_chunks(q), to_chunks(blk), to_chunks(ok), q_pos.reshape(nch, qc)))
    return out.swapaxes(0, 1).reshape(B, Sq, H, -1)


def prompt_attend(pos):
    def attend(q, k, v):
        nblk = -(-k.shape[1] // MOBA_BLOCK)
        kb = block_view(k, nblk)
        vb = block_view(v, nblk)
        return moba_attend(q, pos, block_means(kb),
                           lambda blk: (gather_blocks(kb, blk), gather_blocks(vb, blk)))
    return attend


def sample_attend(pool_k, pool_v, page_table, past, pos):
    def attend(q, k, v):
        B, Sq, H, Dh = k.shape
        k_past = pool_k[page_table].reshape(B, past, H, Dh).astype(k.dtype)
        k_all = jnp.concatenate([k_past, k], axis=1)
        nblk = -(-(past + Sq) // MOBA_BLOCK)
        kb = block_view(k_all, nblk)
        return moba_attend(q, pos, block_means(kb),
                           lambda blk: (gather_blocks(kb, blk),
                                        gather_paged_rows(pool_v, page_table, v, past, blk)))
    return attend


def clamped_swiglu(gu):
    gate, up = gu[..., :D_FF], gu[..., D_FF:]
    gate = jnp.minimum(gate, SWIGLU_LIMIT)
    up = jnp.clip(up, -SWIGLU_LIMIT, SWIGLU_LIMIT)
    return (up + 1.0) * gate * jax.nn.sigmoid(SWIGLU_ALPHA * gate)


def moe_ffn(h, w_router, b_router, w_gu, b_gu, w_dn, b_dn):
    B, S, D = h.shape
    M = B * S
    x = h.reshape(M, D)
    logits = jnp.dot(x, w_router, preferred_element_type=F32) + b_router.astype(F32)
    top_l, top_e = lax.top_k(logits, TOPK_E)
    gate = jax.nn.softmax(top_l, -1)
    A = M * TOPK_E
    blk = MOE_BLOCK if A >= N_EXPERTS * MOE_BLOCK else 8
    n_slot = -(-A // blk) * blk + N_EXPERTS * blk
    n_blocks = n_slot // blk
    e_flat = top_e.reshape(A)
    tok_flat = jnp.repeat(jnp.arange(M, dtype=jnp.int32), TOPK_E)
    g_flat = gate.reshape(A)
    counts = jnp.zeros((N_EXPERTS,), jnp.int32).at[e_flat].add(1)
    padded = (counts + blk - 1) // blk * blk
    pad_end = jnp.cumsum(padded)
    order = jnp.argsort(e_flat)
    e_sorted = e_flat[order]
    rank = jnp.arange(A, dtype=jnp.int32) - (jnp.cumsum(counts) - counts)[e_sorted]
    dest = (pad_end - padded)[e_sorted] + rank
    slot_tok = jnp.full((n_slot,), M, jnp.int32).at[dest].set(tok_flat[order])
    slot_gate = jnp.zeros((n_slot,), F32).at[dest].set(g_flat[order])
    block_e = jnp.minimum(jnp.searchsorted(pad_end, jnp.arange(n_blocks) * blk, side='right'), N_EXPERTS - 1)
    x_slots = jnp.concatenate([x, jnp.zeros((1, D), x.dtype)], 0)[slot_tok].reshape(n_blocks, blk, D)

    def expert_block(args):
        xb, e = args
        gu = xb @ w_gu[e] + b_gu[e]
        return clamped_swiglu(gu) @ w_dn[e] + b_dn[e]

    y = lax.map(expert_block, (x_slots, block_e)).reshape(n_slot, D)
    out = jax.ops.segment_sum(y.astype(F32) * slot_gate[:, None], slot_tok, num_segments=M + 1)[:M]
    return out.astype(h.dtype).reshape(B, S, D)


def trunk_layer(x, c, conv_prev, s_prev, pos, attend, w_in, conv_w, a_log, dt_bias, gdn_norm_w, w_o,
                w_ada, b_ada, ln1_g, ln1_b, ln2_g, ln2_b, w_router, b_router, w_gu, b_gu, w_dn, b_dn):
    B, S, _ = x.shape
    mod = jnp.dot(jax.nn.silu(c), w_ada) + b_ada
    sh1, sc1, gt1, sh2, sc2, gt2 = jnp.split(mod[:, None, :], 6, axis=-1)
    h = x * (1.0 + sc1) + sh1
    proj = h @ w_in
    qkv_a, z_a, b_a, a_a, q_b, k_b, v_b = jnp.split(proj, IN_SPLITS, axis=-1)
    u, conv_new = short_conv(qkv_a, conv_prev, conv_w)
    u = u.reshape(B, S, 3, H_A, HEAD_DIM)
    q_a = l2_norm(u[:, :, 0]) * HEAD_DIM ** -0.5
    k_a = l2_norm(u[:, :, 1])
    v_a = u[:, :, 2]
    beta = jax.nn.sigmoid(b_a.astype(F32))
    g = -jnp.exp(a_log.astype(F32)) * jax.nn.softplus(a_a.astype(F32) + dt_bias.astype(F32))
    o_a, s_new = gdn_chunked(q_a, k_a, v_a, g, beta, s_prev)
    z = jax.nn.silu(z_a.astype(F32)).reshape(B, S, H_A, HEAD_DIM)
    o_a = (rms_norm(o_a, gdn_norm_w) * z).astype(x.dtype).reshape(B, S, W_A)
    q_b = rope(q_b.reshape(B, S, H_B, HEAD_DIM), pos) * HEAD_DIM ** -0.5
    k_b = rope(k_b.reshape(B, S, H_B, HEAD_DIM), pos)
    v_b = v_b.reshape(B, S, H_B, HEAD_DIM)
    o_b = attend(q_b, k_b, v_b).astype(x.dtype).reshape(B, S, W_B)
    mix = jnp.concatenate([o_a, o_b], -1) @ w_o
    x = layer_norm(DN_ALPHA * x + (1.0 + gt1) * mix, ln1_g, ln1_b)
    h2 = x * (1.0 + sc2) + sh2
    x = layer_norm(DN_ALPHA * x + (1.0 + gt2) * moe_ffn(h2, w_router, b_router, w_gu, b_gu, w_dn, b_dn),
                   ln2_g, ln2_b)
    return x, conv_new, s_new, k_b, v_b


def setup_inputs(seed: int = 0) -> dict:
    key = jax.random.key(seed)
    ks = jax.random.split(key, 32)
    n_pages = PAST_LEN // PAGE_SIZE
    n_pool = (5 * DEC_BATCH * n_pages + 3) // 4

    def nrm(k, shape, scale):
        return jax.random.normal(k, shape, F32) * scale

    perm = jax.random.permutation(ks[9], n_pool)
    page_table = perm[:DEC_BATCH * n_pages].reshape(DEC_BATCH, n_pages).astype(jnp.int32)
    dt = jnp.exp(jax.random.uniform(ks[13], (DEPTH, H_A), F32, math.log(1e-3), math.log(1e-1)))
    return {
        'x_prompt': nrm(ks[0], (BATCH, SEQ, D_MODEL), 1.0),
        'x_sample': nrm(ks[1], (DEC_BATCH, DEC_SEQ, D_MODEL), 1.0),
        'cache_k': nrm(ks[2], (DEPTH, n_pool, PAGE_SIZE, H_B, HEAD_DIM), 1.0),
        'cache_v': nrm(ks[3], (DEPTH, n_pool, PAGE_SIZE, H_B, HEAD_DIM), 1.0),
        'state_gdn': nrm(ks[4], (DEPTH, DEC_BATCH, H_A, HEAD_DIM, HEAD_DIM), 0.1),
        'state_conv': nrm(ks[5], (DEPTH, DEC_BATCH, CONV_W - 1, CONV_CH), 1.0),
        'page_table': page_table,
        'c_prompt': nrm(ks[6], (BATCH, D_MODEL), 1.0),
        'c_sample': nrm(ks[7], (DEC_BATCH, D_MODEL), 1.0),
        'w_in': nrm(ks[10], (DEPTH, D_MODEL, IN_COLS), D_MODEL ** -0.5),
        'conv_w': nrm(ks[11], (DEPTH, CONV_W, CONV_CH), 0.5),
        'a_log': jnp.log(jax.random.uniform(ks[12], (DEPTH, H_A), F32, 1.0, 16.0)),
        'dt_bias': dt + jnp.log(-jnp.expm1(-dt)),
        'gdn_norm_w': 1.0 + nrm(ks[14], (DEPTH, HEAD_DIM), 0.02),
        'w_o': nrm(ks[15], (DEPTH, MIX_WIDTH, D_MODEL), DN_BETA * MIX_WIDTH ** -0.5),
        'w_ada': nrm(ks[16], (DEPTH, D_MODEL, 6 * D_MODEL), 0.5 * D_MODEL ** -0.5),
        'b_ada': nrm(ks[17], (DEPTH, 6 * D_MODEL), 0.01),
        'ln1_g': 1.0 + nrm(ks[18], (DEPTH, D_MODEL), 0.02),
        'ln1_b': nrm(ks[19], (DEPTH, D_MODEL), 0.02),
        'ln2_g': 1.0 + nrm(ks[20], (DEPTH, D_MODEL), 0.02),
        'ln2_b': nrm(ks[21], (DEPTH, D_MODEL), 0.02),
        'w_router': nrm(ks[22], (DEPTH, D_MODEL, N_EXPERTS), D_MODEL ** -0.5),
        'b_router': nrm(ks[23], (DEPTH, N_EXPERTS), 0.01),
        'w_gate_up': nrm(ks[24], (DEPTH, N_EXPERTS, D_MODEL, 2 * D_FF), D_MODEL ** -0.5),
        'b_gate_up': nrm(ks[25], (DEPTH, N_EXPERTS, 2 * D_FF), 0.01),
        'w_down': nrm(ks[26], (DEPTH, N_EXPERTS, D_FF, D_MODEL), DN_BETA * D_FF ** -0.5),
        'b_down': nrm(ks[27], (DEPTH, N_EXPERTS, D_MODEL), 0.01),
    }


def reference(x_prompt, x_sample, cache_k, cache_v, state_gdn, state_conv, page_table, c_prompt, c_sample,
              w_in, conv_w, a_log, dt_bias, gdn_norm_w, w_o, w_ada, b_ada, ln1_g, ln1_b, ln2_g, ln2_b,
              w_router, b_router, w_gate_up, b_gate_up, w_down, b_down):
    past = page_table.shape[1] * cache_k.shape[2]
    B = x_prompt.shape[0]
    pos_p = jnp.arange(x_prompt.shape[1], dtype=jnp.int32)
    pos_s = past + jnp.arange(x_sample.shape[1], dtype=jnp.int32)
    conv0 = jnp.zeros((B, CONV_W - 1, CONV_CH), x_prompt.dtype)
    s0 = jnp.zeros((B, H_A, HEAD_DIM, HEAD_DIM), F32)
    xp, xs = x_prompt, x_sample
    kp_l, vp_l, ks_l, vs_l, gp_l, gs_l, cp_l, cs_l = [], [], [], [], [], [], [], []
    for l in range(DEPTH):
        lw = (w_in[l], conv_w[l], a_log[l], dt_bias[l], gdn_norm_w[l], w_o[l], w_ada[l], b_ada[l],
              ln1_g[l], ln1_b[l], ln2_g[l], ln2_b[l], w_router[l], b_router[l],
              w_gate_up[l], b_gate_up[l], w_down[l], b_down[l])
        xp, cp, gp, kp, vp = trunk_layer(xp, c_prompt, conv0, s0, pos_p, prompt_attend(pos_p), *lw)
        xs, cs, gs, ksm, vsm = trunk_layer(xs, c_sample, state_conv[l], state_gdn[l], pos_s,
                                           sample_attend(cache_k[l], cache_v[l], page_table, past, pos_s), *lw)
        kp_l.append(kp); vp_l.append(vp); ks_l.append(ksm); vs_l.append(vsm)
        gp_l.append(gp); gs_l.append(gs); cp_l.append(cp); cs_l.append(cs)
    k_prompt = jnp.stack(kp_l)
    v_prompt = jnp.stack(vp_l)
    k_sample = jnp.stack(ks_l)
    v_sample = jnp.stack(vs_l)
    gdn_prompt = jnp.stack(gp_l)
    gdn_sample = jnp.stack(gs_l)
    conv_prompt = jnp.stack(cp_l)
    conv_sample = jnp.stack(cs_l)
    return (xp, xs, k_prompt, v_prompt, k_sample, v_sample, gdn_prompt, gdn_sample, conv_prompt, conv_sample)
```

```python
import functools

import jax
import jax.numpy as jnp
from jax import lax
from jax.experimental import pallas as pl
from jax.experimental.pallas import tpu as pltpu

F32 = jnp.float32
BF16 = jnp.bfloat16
I32 = jnp.int32

LANES = 128
HEAD_DIM = 128
CONV_W = 4
GDN_CHUNK = 64
MOBA_BLOCK = 256
MOBA_TOPK = 3
ROPE_THETA = 10000.0
TOPK_E = 4
SWIGLU_ALPHA = 1.702
SWIGLU_LIMIT = 7.0
LN_EPS = 1e-5
RMS_EPS = 1e-6
VMEM_LIMIT = 56 * 1024 * 1024
SAMPLE_ROWS = 16
MOE_GROUP = 1024
MOE_SUB = 256
MOE_TF = 256

NN = (((1,), (0,)), ((), ()))
NT = (((1,), (1,)), ((), ()))
TN = (((0,), (0,)), ((), ()))


def _cp(*sem):
    return pltpu.CompilerParams(dimension_semantics=sem, vmem_limit_bytes=VMEM_LIMIT)


def _tile(n, cap, unit):
    return max(t for t in range(unit, min(n, cap) + 1, unit) if n % t == 0)


def _mm(a, b, dims=NN):
    return lax.dot_general(a, b, dims, preferred_element_type=F32)


def _dot1(a, b, dims=NN):
    return _mm(a.astype(BF16), b.astype(BF16), dims)


def _split2(x):
    hi = x.astype(BF16)
    return hi, (x - hi.astype(F32)).astype(BF16)


def _dot3(a, b, dims=NN):
    ah, al = _split2(a)
    bh, bl = _split2(b)
    return _mm(ah, bh, dims) + (_mm(al, bh, dims) + _mm(ah, bl, dims))


def _dot_sel(sel, x, dims=NN):
    hi = x.astype(BF16)
    r = x - hi.astype(F32)
    mid = r.astype(BF16)
    lo = (r - mid.astype(F32)).astype(BF16)
    return _mm(sel, hi, dims) + (_mm(sel, mid, dims) + _mm(sel, lo, dims))


def _sigmoid(x):
    return 1.0 / (1.0 + jnp.exp(-x))


def _softplus(x):
    return jnp.maximum(x, 0.0) + jnp.log(1.0 + jnp.exp(-jnp.abs(x)))


def _layer_norm(y, g, b):
    mu = jnp.mean(y, axis=-1, keepdims=True)
    d = y - mu
    var = jnp.mean(d * d, axis=-1, keepdims=True)
    return d * lax.rsqrt(var + LN_EPS) * g + b


def _ada_kernel(c_ref, w_ref, b_ref, o_ref):
    c = c_ref[...]
    o_ref[0] = _dot3(c * _sigmoid(c), w_ref[0]) + b_ref[0]


def _ada_all(c_all, w_ada, b_ada, tn=1024):
    depth, d, n = w_ada.shape
    r = c_all.shape[0]
    return pl.pallas_call(
        _ada_kernel,
        grid=(depth, n // tn),
        in_specs=[pl.BlockSpec((r, d), lambda l, j: (0, 0)),
                  pl.BlockSpec((1, d, tn), lambda l, j: (l, 0, j)),
                  pl.BlockSpec((1, 1, tn), lambda l, j: (l, 0, j))],
        out_specs=pl.BlockSpec((1, r, tn), lambda l, j: (l, 0, j)),
        out_shape=jax.ShapeDtypeStruct((depth, r, n), F32),
        compiler_params=_cp("parallel", "parallel"),
        name="ada_mod",
    )(c_all, w_ada, b_ada.reshape(depth, 1, n))


def _inproj_kernel(x_ref, sh_ref, sc_ref, w_ref, ws_ref, o_ref, os_ref, h_ref):
    @pl.when(pl.program_id(1) == 0)
    def _():
        h = x_ref[...] * (1.0 + sc_ref[...]) + sh_ref[...]
        h_ref[...] = h.astype(BF16)
        os_ref[...] = _dot3(h, ws_ref[...])

    o_ref[...] = _mm(h_ref[...], w_ref[0])


def _inproj(x2d, mod3, w_main, w_small, layer, tm):
    m, d = x2d.shape
    n = w_main.shape[2]
    tn = _tile(n, 512, LANES)
    groups, r, _ = mod3.shape
    tiles_per_group = m // tm // groups
    mod_spec = lambda chunk: pl.BlockSpec((None, r, d), lambda i, j: (i // tiles_per_group, 0, chunk))
    return pl.pallas_call(
        _inproj_kernel,
        grid=(m // tm, n // tn),
        in_specs=[pl.BlockSpec((tm, d), lambda i, j: (i, 0)),
                  mod_spec(0), mod_spec(1),
                  pl.BlockSpec((1, d, tn), lambda i, j: (layer, 0, j)),
                  pl.BlockSpec((None, d, LANES), lambda i, j: (layer, 0, 0))],
        out_specs=[pl.BlockSpec((tm, tn), lambda i, j: (i, j)),
                   pl.BlockSpec((tm, LANES), lambda i, j: (i, 0))],
        out_shape=[jax.ShapeDtypeStruct((m, n), F32), jax.ShapeDtypeStruct((m, LANES), F32)],
        scratch_shapes=[pltpu.VMEM((tm, d), BF16)],
        compiler_params=_cp("parallel", "arbitrary"),
        name="in_proj",
    )(x2d, mod3, mod3, w_main, w_small)


def _gdn_kernel(qkv_ref, z_ref, sm_ref, cprev_ref, s0_ref, cw_ref, alog_ref, dtb_ref, nw_ref,
                o_ref, s_ref, ext_ref, *, n_heads, n_valid, precise):
    C = GDN_CHUNK
    W = n_heads * HEAD_DIM
    dotp = _dot3 if precise else _dot1

    @pl.when(pl.program_id(1) == 0)
    def _():
        ext_ref[0:8, :] = cprev_ref[0]
        s_ref[...] = s0_ref[...]

    ext_ref[8:8 + C, :] = qkv_ref[...]
    cw = cw_ref[...]
    conv = ext_ref[5:5 + C, :] * cw[0:1, :]
    for i in range(1, CONV_W):
        conv = conv + ext_ref[5 + i:5 + i + C, :] * cw[i:i + 1, :]
    u = conv * _sigmoid(conv)
    ext_ref[0:8, :] = ext_ref[C:C + 8, :]

    sm = sm_ref[...]
    beta_all = _sigmoid(sm)
    g_all = -jnp.exp(alog_ref[...]) * _softplus(sm + dtb_ref[...])
    if n_valid < C:
        valid = lax.broadcasted_iota(I32, (C, LANES), 0) < n_valid
        beta_all = jnp.where(valid, beta_all, 0.0)
        g_all = jnp.where(valid, g_all, 0.0)

    ii = lax.broadcasted_iota(I32, (C, C), 0)
    jj = lax.broadcasted_iota(I32, (C, C), 1)
    tri_incl = ii >= jj
    tri_strict = ii > jj
    tri_bf = jnp.where(tri_incl, 1.0, 0.0).astype(BF16)
    eye = jnp.where(ii == jj, 1.0, 0.0).astype(F32)
    lane0 = jnp.where(lax.broadcasted_iota(I32, (C, LANES), 1) == 0, 1.0, 0.0).astype(BF16)

    for h in range(n_heads):
        sl = slice(h * HEAD_DIM, (h + 1) * HEAD_DIM)
        qh = u[:, h * HEAD_DIM:(h + 1) * HEAD_DIM]
        kh = u[:, W + h * HEAD_DIM:W + (h + 1) * HEAD_DIM]
        vh = u[:, 2 * W + h * HEAD_DIM:2 * W + (h + 1) * HEAD_DIM]
        qn = qh * lax.rsqrt(jnp.sum(qh * qh, axis=-1, keepdims=True) + 1e-6) * (HEAD_DIM ** -0.5)
        kn = kh * lax.rsqrt(jnp.sum(kh * kh, axis=-1, keepdims=True) + 1e-6)
        beta = beta_all[:, h:h + 1]
        gb = jnp.broadcast_to(g_all[:, n_heads + h:n_heads + h + 1], (C, LANES))
        gc = _dot_sel(tri_bf, gb)
        grow = _dot_sel(lane0, gc, NT)
        decay = jnp.where(tri_incl, jnp.exp(jnp.where(tri_incl, gc[:, :C] - grow, 0.0)), 0.0)
        eg = jnp.exp(gc)
        kk = dotp(kn, kn, NT)
        a = jnp.where(tri_strict, -(beta * kk * decay), 0.0)
        t = eye + a
        p = a
        for _ in range(C.bit_length() - 2):
            p = _dot3(p, p)
            t = t + _dot3(t, p)
        sol = _dot3(t, jnp.concatenate([beta * vh, beta * eg * kn], axis=1))
        s_old = s_ref[0, h]
        un = sol[:, :HEAD_DIM] - dotp(sol[:, HEAD_DIM:], s_old)
        qk = dotp(qn, kn, NT) * decay
        o = dotp(qn * eg, s_old) + dotp(qk, un)
        glast = gc[C - 1:C, :]
        s_ref[0, h] = jnp.exp(glast) * s_old + dotp(kn * jnp.exp(glast - gc), un, TN)
        on = o * lax.rsqrt(jnp.mean(o * o, axis=-1, keepdims=True) + RMS_EPS) * nw_ref[...]
        zh = z_ref[:, sl]
        o_ref[:, sl] = (on * (zh * _sigmoid(zh))).astype(o_ref.dtype)


def _gdn(proj, small, conv_prev8, s0, conv_w, alog_row, dtb_row, norm_w, n_heads, n_valid, precise):
    C = GDN_CHUNK
    W = n_heads * HEAD_DIM
    B = s0.shape[0]
    nc = proj.shape[0] // B // C
    kern = functools.partial(_gdn_kernel, n_heads=n_heads, n_valid=n_valid, precise=precise)
    full = lambda shape: pl.BlockSpec(shape, lambda b, c: (0,) * len(shape))
    return pl.pallas_call(
        kern,
        grid=(B, nc),
        in_specs=[pl.BlockSpec((C, 3 * W), lambda b, c: (b * nc + c, 0)),
                  pl.BlockSpec((C, W), lambda b, c: (b * nc + c, 3)),
                  pl.BlockSpec((C, LANES), lambda b, c: (b * nc + c, 0)),
                  pl.BlockSpec((1, 8, 3 * W), lambda b, c: (b, 0, 0)),
                  pl.BlockSpec((1, n_heads, HEAD_DIM, HEAD_DIM), lambda b, c: (b, 0, 0, 0)),
                  full((CONV_W, 3 * W)), full((1, LANES)), full((1, LANES)), full((1, HEAD_DIM))],
        out_specs=[pl.BlockSpec((C, W), lambda b, c: (b * nc + c, 0)),
                   pl.BlockSpec((1, n_heads, HEAD_DIM, HEAD_DIM), lambda b, c: (b, 0, 0, 0))],
        out_shape=[jax.ShapeDtypeStruct((B * nc * C, W), BF16),
                   jax.ShapeDtypeStruct((B, n_heads, HEAD_DIM, HEAD_DIM), F32)],
        scratch_shapes=[pltpu.VMEM((C + 8, 3 * W), F32)],
        compiler_params=_cp("parallel", "arbitrary"),
        name="gdn",
    )(proj, proj, small, conv_prev8, s0, conv_w, alog_row, dtb_row, norm_w)


def _rope(x, cos, sin):
    return x * cos + pltpu.roll(x, HEAD_DIM // 2, 1) * sin


def _rope_kernel(q_ref, k_ref, cos_ref, sin_ref, qo_ref, ko_ref, km_ref, *, n_heads):
    cos = cos_ref[...]
    sin = sin_ref[...]
    for h in range(n_heads):
        sl = slice(h * HEAD_DIM, (h + 1) * HEAD_DIM)
        qo_ref[:, sl] = _rope(q_ref[:, sl], cos, sin) * (HEAD_DIM ** -0.5)
        kr = _rope(k_ref[:, sl], cos, sin)
        ko_ref[:, sl] = kr
        km_ref[0, :, sl] = jnp.sum(kr, axis=0, keepdims=True) * (1.0 / MOBA_BLOCK)


def _rope_prompt(proj, cos, sin, n_heads, B):
    W = n_heads * HEAD_DIM
    m = proj.shape[0]
    nb = m // B // MOBA_BLOCK
    return pl.pallas_call(
        functools.partial(_rope_kernel, n_heads=n_heads),
        grid=(B, nb),
        in_specs=[pl.BlockSpec((MOBA_BLOCK, W), lambda b, i: (b * nb + i, 4)),
                  pl.BlockSpec((MOBA_BLOCK, W), lambda b, i: (b * nb + i, 5)),
                  pl.BlockSpec((MOBA_BLOCK, HEAD_DIM), lambda b, i: (i, 0)),
                  pl.BlockSpec((MOBA_BLOCK, HEAD_DIM), lambda b, i: (i, 0))],
        out_specs=[pl.BlockSpec((MOBA_BLOCK, W), lambda b, i: (b * nb + i, 0)),
                   pl.BlockSpec((MOBA_BLOCK, W), lambda b, i: (b * nb + i, 0)),
                   pl.BlockSpec((1, 1, W), lambda b, i: (b * nb + i, 0, 0))],
        out_shape=[jax.ShapeDtypeStruct((m, W), F32), jax.ShapeDtypeStruct((m, W), F32),
                   jax.ShapeDtypeStruct((B * nb, 1, W), F32)],
        compiler_params=_cp("parallel", "parallel"),
        name="rope_prompt",
    )(proj, proj, cos, sin)


def _moba_kernel(q_ref, k_ref, v_ref, km_ref, o_ref, *, nb):
    i = pl.program_id(2)
    tq = MOBA_BLOCK
    S = nb * MOBA_BLOCK
    q = q_ref[...]
    km = jnp.concatenate([km_ref[0], jnp.zeros((LANES - nb, HEAD_DIM), F32)], axis=0)
    s = _dot3(q, km, NT)
    lane = lax.broadcasted_iota(I32, (tq, LANES), 1)
    past = lane < i
    s = jnp.where(past, s, -jnp.inf)
    rank = jnp.zeros((tq, LANES), I32)
    for j in range(nb):
        sj = s[:, j:j + 1]
        rank = rank + jnp.where((sj > s) | ((sj == s) & (j < lane)), 1, 0)
    sel = jnp.where(past & (rank < MOBA_TOPK), 1.0, 0.0).astype(BF16)
    expand = jnp.where(lax.broadcasted_iota(I32, (LANES, S), 1) // MOBA_BLOCK
                       == lax.broadcasted_iota(I32, (LANES, S), 0), 1.0, 0.0).astype(BF16)
    chosen = _mm(sel, expand) > 0.5
    col = lax.broadcasted_iota(I32, (tq, S), 1)
    row = i * tq + lax.broadcasted_iota(I32, (tq, S), 0)
    own = (col // MOBA_BLOCK == i) & (col <= row)
    logits = jnp.where(chosen | own, _dot1(q, k_ref[...], NT), -jnp.inf)
    mx = jnp.max(logits, axis=-1, keepdims=True)
    p = jnp.exp(logits - mx)
    o = _dot1(p, v_ref[...]) / jnp.sum(p, axis=-1, keepdims=True)
    o_ref[...] = o.astype(o_ref.dtype)


def _moba_prompt(q_rot, k_rot, proj, kmean, n_heads, B):
    m, W = q_rot.shape
    S = m // B
    nb = S // MOBA_BLOCK
    return pl.pallas_call(
        functools.partial(_moba_kernel, nb=nb),
        grid=(B, n_heads, nb),
        in_specs=[pl.BlockSpec((MOBA_BLOCK, HEAD_DIM), lambda b, h, i: (b * nb + i, h)),
                  pl.BlockSpec((S, HEAD_DIM), lambda b, h, i: (b, h)),
                  pl.BlockSpec((S, HEAD_DIM), lambda b, h, i: (b, 6 * n_heads + h)),
                  pl.BlockSpec((1, nb, HEAD_DIM), lambda b, h, i: (b, 0, h))],
        out_specs=pl.BlockSpec((MOBA_BLOCK, HEAD_DIM), lambda b, h, i: (b * nb + i, h)),
        out_shape=jax.ShapeDtypeStruct((m, W), BF16),
        compiler_params=_cp("parallel", "parallel", "arbitrary"),
        name="moba_prompt",
    )(q_rot, k_rot, proj, kmean.reshape(B, nb, W))


def _rope_s_kernel(q_ref, k_ref, cos_ref, sin_ref, qo_ref, ko_ref, *, n_heads):
    cos = cos_ref[...]
    sin = sin_ref[...]
    for h in range(n_heads):
        sl = slice(h * HEAD_DIM, (h + 1) * HEAD_DIM)
        qo_ref[:, sl] = _rope(q_ref[:, sl], cos, sin) * (HEAD_DIM ** -0.5)
        ko_ref[:, sl] = _rope(k_ref[:, sl], cos, sin)


def _rope_sample(proj_s, cos, sin, n_heads):
    W = n_heads * HEAD_DIM
    r = proj_s.shape[0]
    return pl.pallas_call(
        functools.partial(_rope_s_kernel, n_heads=n_heads),
        grid=(1,),
        in_specs=[pl.BlockSpec((r, W), lambda i: (0, 4)), pl.BlockSpec((r, W), lambda i: (0, 5)),
                  pl.BlockSpec((1, HEAD_DIM), lambda i: (0, 0)), pl.BlockSpec((1, HEAD_DIM), lambda i: (0, 0))],
        out_specs=[pl.BlockSpec((r, W), lambda i: (0, 0)), pl.BlockSpec((r, W), lambda i: (0, 0))],
        out_shape=[jax.ShapeDtypeStruct((r, W), F32), jax.ShapeDtypeStruct((r, W), F32)],
        compiler_params=_cp("arbitrary"),
        name="rope_sample",
    )(proj_s, proj_s, cos, sin)


def _pool_mean_kernel(pt_ref, *refs, pages_per_step, pages_per_block):
    o_ref = refs[pages_per_step]
    for n in range(pages_per_step // pages_per_block):
        acc = jnp.sum(refs[n * pages_per_block][0, 0], axis=0, keepdims=True)
        for t in range(1, pages_per_block):
            acc = acc + jnp.sum(refs[n * pages_per_block + t][0, 0], axis=0, keepdims=True)
        o_ref[0, n:n + 1, :] = acc * (1.0 / MOBA_BLOCK)


def _pool_block_means(pool_k4, page_table, layer, pages_per_step=8):
    _, _, page, W = pool_k4.shape
    B, n_pages = page_table.shape
    ppb = MOBA_BLOCK // page
    bps = pages_per_step // ppb
    nblk = n_pages // ppb
    page_spec = lambda t: pl.BlockSpec(
        (1, 1, page, W), lambda b, s, pt: (layer, pt[b, s * pages_per_step + t], 0, 0))
    return pl.pallas_call(
        functools.partial(_pool_mean_kernel, pages_per_step=pages_per_step, pages_per_block=ppb),
        grid_spec=pltpu.PrefetchScalarGridSpec(
            num_scalar_prefetch=1,
            grid=(B, n_pages // pages_per_step),
            in_specs=[page_spec(t) for t in range(pages_per_step)],
            out_specs=pl.BlockSpec((1, bps, W), lambda b, s, pt: (b * (nblk // bps) + s, 0, 0)),
        ),
        out_shape=jax.ShapeDtypeStruct((B * nblk // bps, bps, W), F32),
        compiler_params=_cp("parallel", "arbitrary"),
        name="pool_block_means",
    )(page_table, *([pool_k4] * pages_per_step)).reshape(B, nblk, W)


def _sample_select_kernel(q_ref, km_ref, sel_ref, *, n_heads, nblk):
    q = q_ref[0]
    rowi = lax.broadcasted_iota(I32, (n_heads, LANES), 0)
    lane = lax.broadcasted_iota(I32, (n_heads, LANES), 1)
    s = jnp.full((n_heads, LANES), -jnp.inf, F32)
    for h in range(n_heads):
        kmh = jnp.concatenate([km_ref[0, :, h * HEAD_DIM:(h + 1) * HEAD_DIM],
                               jnp.zeros((LANES - nblk, HEAD_DIM), F32)], axis=0)
        s = jnp.where((rowi == h) & (lane < nblk), _dot3(q, kmh, NT), s)
    out = jnp.zeros((n_heads, LANES), I32)
    for t in range(MOBA_TOPK):
        mx = jnp.max(s, axis=-1, keepdims=True)
        idx = jnp.min(jnp.where(s == mx, lane, LANES), axis=-1, keepdims=True)
        out = jnp.where(lane == t, idx, out)
        s = jnp.where(lane == idx, -jnp.inf, s)
    sel_ref[0] = out


def _sample_select(q3, kmean_s):
    B, n_heads, _ = q3.shape
    nblk, W = kmean_s.shape[1:]
    return pl.pallas_call(
        functools.partial(_sample_select_kernel, n_heads=n_heads, nblk=nblk),
        grid=(B,),
        in_specs=[pl.BlockSpec((1, n_heads, HEAD_DIM), lambda b: (b, 0, 0)),
                  pl.BlockSpec((1, nblk, W), lambda b: (b, 0, 0))],
        out_specs=pl.BlockSpec((1, n_heads, LANES), lambda b: (b, 0, 0)),
        out_shape=jax.ShapeDtypeStruct((B, n_heads, LANES), I32),
        compiler_params=_cp("parallel"),
        name="sample_select",
    )(q3, kmean_s)


def _sample_attn_kernel(pg_ref, q_ref, kp_ref, vp_ref, kn_ref, vn_ref, o_ref, m_ref, l_ref, acc_ref, *, n_steps):
    h = pl.program_id(1)
    t = pl.program_id(2)
    q = q_ref[0, pl.ds(h, 1), :]

    @pl.when(t == 0)
    def _():
        m_ref[...] = jnp.broadcast_to(jnp.sum(q * kn_ref[0, pl.ds(h, 1), :], axis=-1, keepdims=True), m_ref.shape)
        l_ref[...] = jnp.ones(l_ref.shape, F32)
        acc_ref[...] = vn_ref[0, pl.ds(h, 1), :]

    q8 = jnp.broadcast_to(q, (8, HEAD_DIM))
    s = _dot3(q8, kp_ref[0, 0], NT)[0:1, :]
    m_old = m_ref[...]
    m_new = jnp.maximum(m_old, jnp.max(s, axis=-1, keepdims=True))
    alpha = jnp.exp(m_old - m_new)
    p = jnp.exp(s - m_new)
    l_ref[...] = alpha * l_ref[...] + jnp.sum(p, axis=-1, keepdims=True)
    acc_ref[...] = alpha * acc_ref[...] + _dot3(jnp.broadcast_to(p, (8, p.shape[1])), vp_ref[0, 0])[0:1, :]
    m_ref[...] = m_new

    @pl.when(t == n_steps - 1)
    def _():
        o_ref[0, pl.ds(h, 1), :] = acc_ref[...] / l_ref[...]


def _sample_attn(pages_flat, q3, pool_k4, pool_v4, kn3, vn3, layer, n_steps):
    B, n_heads, _ = q3.shape
    page = pool_k4.shape[2]
    assert page == LANES
    pool_spec = pl.BlockSpec((1, 1, page, HEAD_DIM),
                             lambda b, h, t, pg: (layer, pg[(b * n_heads + h) * n_steps + t], 0, h))
    tok_spec = pl.BlockSpec((1, n_heads, HEAD_DIM), lambda b, h, t, pg: (b, 0, 0))
    return pl.pallas_call(
        functools.partial(_sample_attn_kernel, n_steps=n_steps),
        grid_spec=pltpu.PrefetchScalarGridSpec(
            num_scalar_prefetch=1,
            grid=(B, n_heads, n_steps),
            in_specs=[tok_spec, pool_spec, pool_spec, tok_spec, tok_spec],
            out_specs=tok_spec,
            scratch_shapes=[pltpu.VMEM((1, LANES), F32), pltpu.VMEM((1, LANES), F32), pltpu.VMEM((1, HEAD_DIM), F32)],
        ),
        out_shape=jax.ShapeDtypeStruct((B, n_heads, HEAD_DIM), F32),
        compiler_params=_cp("parallel", "arbitrary", "arbitrary"),
        name="sample_attn",
    )(pages_flat, q3, pool_k4, pool_v4, kn3, vn3)


def _oproj_kernel(oa_ref, ob_ref, wa_ref, wb_ref, x_ref, gt_ref, sc_ref, sh_ref, g_ref, b_ref, wr_ref, br_ref,
                  x1_ref, h2_ref, lg_ref, *, alpha):
    mix = _mm(oa_ref[...], wa_ref[0]) + _mm(ob_ref[...], wb_ref[0])
    x1 = _layer_norm(alpha * x_ref[...] + (1.0 + gt_ref[...]) * mix, g_ref[0], b_ref[0])
    x1_ref[...] = x1
    h2 = x1 * (1.0 + sc_ref[...]) + sh_ref[...]
    h2_ref[...] = h2.astype(BF16)
    lg_ref[...] = _dot3(h2, wr_ref[0]) + br_ref[0]


def _oproj(o_a, o_b, w_o_bf, x2d, mod3, ln_g, ln_b, w_router, b_router, layer, tm, alpha):
    m, d = x2d.shape
    W = o_a.shape[1]
    groups, r, _ = mod3.shape
    tiles_per_group = m // tm // groups
    mod_spec = lambda chunk: pl.BlockSpec((None, r, d), lambda i: (i // tiles_per_group, 0, chunk))
    lyr = lambda shape: pl.BlockSpec(shape, lambda i: (layer,) + (0,) * (len(shape) - 1))
    return pl.pallas_call(
        functools.partial(_oproj_kernel, alpha=alpha),
        grid=(m // tm,),
        in_specs=[pl.BlockSpec((tm, W), lambda i: (i, 0)), pl.BlockSpec((tm, W), lambda i: (i, 0)),
                  pl.BlockSpec((1, W, d), lambda i: (layer, 0, 0)), pl.BlockSpec((1, W, d), lambda i: (layer, 1, 0)),
                  pl.BlockSpec((tm, d), lambda i: (i, 0)),
                  mod_spec(2), mod_spec(4), mod_spec(3),
                  lyr((1, 1, d)), lyr((1, 1, d)), lyr((1, d, LANES)), lyr((1, 1, LANES))],
        out_specs=[pl.BlockSpec((tm, d), lambda i: (i, 0)), pl.BlockSpec((tm, d), lambda i: (i, 0)),
                   pl.BlockSpec((tm, LANES), lambda i: (i, 0))],
        out_shape=[jax.ShapeDtypeStruct((m, d), F32), jax.ShapeDtypeStruct((m, d), BF16),
                   jax.ShapeDtypeStruct((m, LANES), F32)],
        compiler_params=_cp("parallel"),
        name="out_proj",
    )(o_a, o_b, w_o_bf, w_o_bf, x2d, mod3, mod3, mod3, ln_g, ln_b, w_router, b_router)


def _route_kernel(lg_ref, e_ref, g_ref, *, n_exp):
    shape = lg_ref.shape
    lane = lax.broadcasted_iota(I32, shape, 1)
    l = jnp.where(lane < n_exp, lg_ref[...], -jnp.inf)
    vals, idxs = [], []
    for _ in range(TOPK_E):
        mx = jnp.max(l, axis=-1, keepdims=True)
        idx = jnp.min(jnp.where(l == mx, lane, LANES), axis=-1, keepdims=True)
        vals.append(mx)
        idxs.append(idx)
        l = jnp.where(lane == idx, -jnp.inf, l)
    ex = [jnp.exp(v - vals[0]) for v in vals]
    tot = ex[0]
    for e in ex[1:]:
        tot = tot + e
    e_out = jnp.zeros(shape, I32)
    g_out = jnp.zeros(shape, F32)
    for k in range(TOPK_E):
        e_out = jnp.where(lane == k, idxs[k], e_out)
        g_out = jnp.where(lane == k, ex[k] / tot, g_out)
    e_ref[...] = e_out
    g_ref[...] = g_out


def _route(logits, n_exp, tm):
    m = logits.shape[0]
    spec = pl.BlockSpec((tm, LANES), lambda i: (i, 0))
    return pl.pallas_call(
        functools.partial(_route_kernel, n_exp=n_exp),
        grid=(m // tm,),
        in_specs=[spec], out_specs=[spec, spec],
        out_shape=[jax.ShapeDtypeStruct((m, LANES), I32), jax.ShapeDtypeStruct((m, LANES), F32)],
        compiler_params=_cp("parallel"),
        name="route_top4",
    )(logits)


def _moe_kernel(ge_ref, gr_ref, gb_ref, x_ref, wg_ref, wu_ref, bg_ref, bu_ref, wd_ref, bd_ref, y_ref):
    g = pl.program_id(0)
    j = pl.program_id(1)
    rows = gr_ref[g]

    @pl.when(rows > 0)
    def _():
        wg = wg_ref[0, 0].astype(BF16)
        wu = wu_ref[0, 0].astype(BF16)
        wd = wd_ref[0, 0].astype(BF16)
        for r in range(MOE_GROUP // MOE_SUB):
            rs = slice(r * MOE_SUB, (r + 1) * MOE_SUB)

            @pl.when(r * MOE_SUB < rows)
            def _():
                xb = x_ref[rs, :]
                gate = jnp.minimum(_mm(xb, wg) + bg_ref[0, 0], SWIGLU_LIMIT)
                up = jnp.clip(_mm(xb, wu) + bu_ref[0, 0], -SWIGLU_LIMIT, SWIGLU_LIMIT)
                act = (up + 1.0) * gate * _sigmoid(SWIGLU_ALPHA * gate)
                part = _mm(act.astype(BF16), wd)

                @pl.when(j == 0)
                def _():
                    y_ref[rs, :] = part + bd_ref[0, 0]

                @pl.when(j > 0)
                def _():
                    y_ref[rs, :] += part

            @pl.when((r * MOE_SUB >= rows) & (j == 0))
            def _():
                y_ref[rs, :] = jnp.zeros((MOE_SUB, y_ref.shape[1]), F32)


def _moe_experts(group_e, group_rows, group_blk, xs, w_gu, b_gu4, w_dn, b_dn4, layer):
    n_groups = group_e.shape[0]
    d = xs.shape[1]
    f = w_dn.shape[2]
    nf = f // MOE_TF
    jx = lambda j, gr, g: jnp.where(gr[g] > 0, j, nf - 1)
    return pl.pallas_call(
        _moe_kernel,
        grid_spec=pltpu.PrefetchScalarGridSpec(
            num_scalar_prefetch=3,
            grid=(n_groups, nf),
            in_specs=[
                pl.BlockSpec((MOE_GROUP, d), lambda g, j, ge, gr, gb: (gb[g], 0)),
                pl.BlockSpec((1, 1, d, MOE_TF), lambda g, j, ge, gr, gb: (layer, ge[g], 0, jx(j, gr, g))),
                pl.BlockSpec((1, 1, d, MOE_TF), lambda g, j, ge, gr, gb: (layer, ge[g], 0, nf + jx(j, gr, g))),
                pl.BlockSpec((1, 1, 1, MOE_TF), lambda g, j, ge, gr, gb: (layer, ge[g], 0, jx(j, gr, g))),
                pl.BlockSpec((1, 1, 1, MOE_TF), lambda g, j, ge, gr, gb: (layer, ge[g], 0, nf + jx(j, gr, g))),
                pl.BlockSpec((1, 1, MOE_TF, d), lambda g, j, ge, gr, gb: (layer, ge[g], jx(j, gr, g), 0)),
                pl.BlockSpec((1, 1, 1, d), lambda g, j, ge, gr, gb: (layer, ge[g], 0, 0)),
            ],
            out_specs=pl.BlockSpec((MOE_GROUP, d), lambda g, j, ge, gr, gb: (gb[g], 0)),
        ),
        out_shape=jax.ShapeDtypeStruct((n_groups * MOE_GROUP, d), F32),
        compiler_params=_cp("arbitrary", "arbitrary"),
        name="moe_experts",
    )(group_e, group_rows, group_blk, xs, w_gu, w_gu, b_gu4, b_gu4, w_dn, b_dn4)


def _dispatch(top_e, n_exp):
    m = top_e.shape[0]
    a = m * TOPK_E
    R = MOE_GROUP
    n_groups = -(-a // R) + n_exp
    e_flat = top_e.reshape(a)
    onehot = (e_flat[:, None] == jnp.arange(n_exp, dtype=I32)[None, :]).astype(I32)
    csum = jnp.cumsum(onehot, axis=0)
    rank = jnp.take_along_axis(csum, e_flat[:, None], axis=1)[:, 0] - 1
    counts = csum[-1]
    ng_e = (counts + R - 1) // R
    g_end = jnp.cumsum(ng_e)
    g_start = g_end - ng_e
    dest = g_start[e_flat] * R + rank
    gid = jnp.arange(n_groups, dtype=I32)
    used = gid < g_end[-1]
    last = jnp.maximum(g_end[-1] - 1, 0)
    ge = jnp.minimum(jnp.searchsorted(g_end, jnp.minimum(gid, last), side="right"), n_exp - 1).astype(I32)
    rows = jnp.where(used, jnp.clip(counts[ge] - (gid - g_start[ge]) * R, 0, R), 0).astype(I32)
    blk = jnp.where(used, gid, last).astype(I32)
    tok = jnp.arange(a, dtype=I32) // TOPK_E
    slot_tok = jnp.full((n_groups * R,), m, I32).at[dest].set(tok)
    return dest.reshape(m, TOPK_E), slot_tok, ge, rows, blk


def _final_kernel(x1_ref, moe_ref, gt_ref, g_ref, b_ref, o_ref, *, alpha):
    o_ref[...] = _layer_norm(alpha * x1_ref[...] + (1.0 + gt_ref[...]) * moe_ref[...], g_ref[0], b_ref[0])


def _final(x1, moe, mod3, ln_g, ln_b, layer, tm, alpha):
    m, d = x1.shape
    groups, r, _ = mod3.shape
    tiles_per_group = m // tm // groups
    row = pl.BlockSpec((tm, d), lambda i: (i, 0))
    lyr = pl.BlockSpec((1, 1, d), lambda i: (layer, 0, 0))
    return pl.pallas_call(
        functools.partial(_final_kernel, alpha=alpha),
        grid=(m // tm,),
        in_specs=[row, row, pl.BlockSpec((None, r, d), lambda i: (i // tiles_per_group, 0, 5)), lyr, lyr],
        out_specs=row,
        out_shape=jax.ShapeDtypeStruct((m, d), F32),
        compiler_params=_cp("parallel"),
        name="final_ln",
    )(x1, moe, mod3, ln_g, ln_b)


def _rope_tables(pos):
    half = HEAD_DIM // 2
    inv = ROPE_THETA ** (-2.0 * jnp.arange(half, dtype=F32) / HEAD_DIM)
    ang = pos.astype(F32)[:, None] * inv[None, :]
    cos, sin = jnp.cos(ang), jnp.sin(ang)
    return jnp.concatenate([cos, cos], -1), jnp.concatenate([-sin, sin], -1)


def _lane_row(v, offset):
    return jnp.zeros((1, LANES), F32).at[0, offset:offset + v.shape[0]].set(v)


def kernel(x_prompt, x_sample, cache_k, cache_v, state_gdn, state_conv, page_table, c_prompt, c_sample,
           w_in, conv_w, a_log, dt_bias, gdn_norm_w, w_o, w_ada, b_ada, ln1_g, ln1_b, ln2_g, ln2_b,
           w_router, b_router, w_gate_up, b_gate_up, w_down, b_down):
    B, S, D = x_prompt.shape
    Bs, Ss, _ = x_sample.shape
    depth = w_in.shape[0]
    H = a_log.shape[1]
    W = H * HEAD_DIM
    n_exp = w_router.shape[2]
    page = cache_k.shape[2]
    n_pages = page_table.shape[1]
    past = n_pages * page
    RS = SAMPLE_ROWS
    assert Ss == 1 and Bs <= 8 and S % MOBA_BLOCK == 0 and S >= CONV_W - 1
    assert past % MOBA_BLOCK == 0 and past // MOBA_BLOCK >= MOBA_TOPK and MOBA_BLOCK % page == 0
    assert n_exp <= LANES and 2 * H <= LANES and cache_k.shape[3] == H and w_in.shape[2] == 7 * W + 2 * H
    alpha = (2 * depth) ** 0.25
    MP = B * S

    w_main = jnp.concatenate([w_in[:, :, :4 * W], w_in[:, :, 4 * W + 2 * H:]], axis=-1).astype(BF16)
    w_small = jnp.pad(w_in[:, :, 4 * W:4 * W + 2 * H], ((0, 0), (0, 0), (0, LANES - 2 * H)))
    w_o_bf = w_o.astype(BF16)
    w_router_p = jnp.pad(w_router, ((0, 0), (0, 0), (0, LANES - n_exp)))
    b_router_p = jnp.pad(b_router, ((0, 0), (0, LANES - n_exp))).reshape(depth, 1, LANES)
    ln1_g3, ln1_b3 = ln1_g.reshape(depth, 1, D), ln1_b.reshape(depth, 1, D)
    ln2_g3, ln2_b3 = ln2_g.reshape(depth, 1, D), ln2_b.reshape(depth, 1, D)
    b_gu4 = b_gate_up.reshape(depth, n_exp, 1, -1)
    b_dn4 = b_down.reshape(depth, n_exp, 1, D)
    pool_k4 = cache_k.reshape(depth, -1, page, W)
    pool_v4 = cache_v.reshape(depth, -1, page, W)
    cos_p, sin_p = _rope_tables(jnp.arange(S, dtype=I32))
    cos_s, sin_s = _rope_tables(jnp.full((1,), past, I32))

    c_all = jnp.concatenate([c_prompt, jnp.zeros((8 - B % 8 if B % 8 else 0, D), F32),
                             c_sample, jnp.zeros((RS - Bs, D), F32)], axis=0)
    s_row0 = c_all.shape[0] - RS
    mod_all = _ada_all(c_all, w_ada, b_ada)

    xp = x_prompt.reshape(MP, D)
    xs = jnp.pad(x_sample.reshape(Bs, D), ((0, RS - Bs), (0, 0)))
    zero_conv = jnp.zeros((B, 8, 3 * W), F32)
    zero_state = jnp.zeros((B, H, HEAD_DIM, HEAD_DIM), F32)
    outs = [[] for _ in range(8)]
    ppb = MOBA_BLOCK // page

    for l in range(depth):
        mod_p = mod_all[l, :B].reshape(B, 1, 6 * D)
        mod_s = mod_all[l, s_row0:].reshape(1, RS, 6 * D)
        alog_row = _lane_row(a_log[l], H)
        dtb_row = _lane_row(dt_bias[l], H)
        nw = gdn_norm_w[l].reshape(1, HEAD_DIM)

        proj_p, small_p = _inproj(xp, mod_p, w_main, w_small, l, tm=_tile(S, 1024, 8))
        oa_p, gdn_p = _gdn(proj_p, small_p, zero_conv, zero_state, conv_w[l], alog_row, dtb_row, nw,
                           H, GDN_CHUNK, False)
        q_rot, k_rot, kmean = _rope_prompt(proj_p, cos_p, sin_p, H, B)
        ob_p = _moba_prompt(q_rot, k_rot, proj_p, kmean, H, B)

        proj_s, small_s = _inproj(xs, mod_s, w_main, w_small, l, tm=RS)
        pad_rows = lambda t: jnp.pad(t[:Bs, None, :], ((0, 0), (0, GDN_CHUNK - 1), (0, 0))).reshape(Bs * GDN_CHUNK, -1)
        conv_prev8 = jnp.pad(state_conv[l], ((0, 0), (8 - (CONV_W - 1), 0), (0, 0)))
        oa_s64, gdn_s = _gdn(pad_rows(proj_s), pad_rows(small_s), conv_prev8, state_gdn[l], conv_w[l],
                             alog_row, dtb_row, nw, H, 1, True)
        oa_s = jnp.pad(oa_s64[::GDN_CHUNK], ((0, RS - Bs), (0, 0)))
        qs_rot, ks_rot = _rope_sample(proj_s, cos_s, sin_s, H)
        q3 = qs_rot[:Bs].reshape(Bs, H, HEAD_DIM)
        kn3 = ks_rot[:Bs].reshape(Bs, H, HEAD_DIM)
        vn3 = proj_s[:Bs, 6 * W:7 * W].reshape(Bs, H, HEAD_DIM)
        kmean_s = _pool_block_means(pool_k4, page_table, l)
        sel = _sample_select(q3, kmean_s)[:, :, :MOBA_TOPK]
        lpage = sel[..., None] * ppb + jnp.arange(ppb, dtype=I32)
        pages = jnp.take_along_axis(page_table, lpage.reshape(Bs, -1), axis=1)
        ob_s3 = _sample_attn(pages.reshape(-1).astype(I32), q3, pool_k4, pool_v4, kn3, vn3, l, MOBA_TOPK * ppb)
        ob_s = jnp.pad(ob_s3.reshape(Bs, W), ((0, RS - Bs), (0, 0))).astype(BF16)

        x1_p, h2_p, lg_p = _oproj(oa_p, ob_p, w_o_bf, xp, mod_p, ln1_g3, ln1_b3, w_router_p, b_router_p, l, 256, alpha)
        x1_s, h2_s, lg_s = _oproj(oa_s, ob_s, w_o_bf, xs, mod_s, ln1_g3, ln1_b3, w_router_p, b_router_p, l, RS, alpha)

        lg_all = jnp.concatenate([lg_p, lg_s], axis=0)
        m_all = MP + RS
        top_e, gates = _route(lg_all, n_exp, _tile(m_all, 1024, 8))
        dest, slot_tok, ge, rows, blk = _dispatch(top_e[:, :TOPK_E], n_exp)
        h2_all = jnp.concatenate([h2_p, h2_s, jnp.zeros((8, D), BF16)], axis=0)
        y = _moe_experts(ge, rows, blk, h2_all[slot_tok], w_gate_up, b_gu4, w_down, b_dn4, l)
        moe = jnp.sum(y[dest] * gates[:, :TOPK_E, None], axis=1)

        xp = _final(x1_p, moe[:MP], mod_p, ln2_g3, ln2_b3, l, _tile(S, 512, 8), alpha)
        xs = _final(x1_s, moe[MP:], mod_s, ln2_g3, ln2_b3, l, RS, alpha)

        outs[0].append(k_rot.reshape(B, S, H, HEAD_DIM))
        outs[1].append(proj_p[:, 6 * W:7 * W].reshape(B, S, H, HEAD_DIM))
        outs[2].append(kn3.reshape(Bs, 1, H, HEAD_DIM))
        outs[3].append(vn3.reshape(Bs, 1, H, HEAD_DIM))
        outs[4].append(gdn_p)
        outs[5].append(gdn_s)
        outs[6].append(proj_p.reshape(B, S, -1)[:, S - (CONV_W - 1):, :3 * W])
        outs[7].append(jnp.concatenate([state_conv[l][:, 1:], proj_s[:Bs, None, :3 * W]], axis=1))

    return (xp.reshape(B, S, D), xs[:Bs].reshape(Bs, 1, D)) + tuple(jnp.stack(o) for o in outs)
```

```python
import functools

import jax
import jax.numpy as jnp
from jax import lax
from jax.experimental import pallas as pl
from jax.experimental.pallas import tpu as pltpu

F32 = jnp.float32
BF16 = jnp.bfloat16
I32 = jnp.int32

LANES = 128
HEAD_DIM = 128
CONV_W = 4
GDN_CHUNK = 64
MOBA_BLOCK = 256
MOBA_TOPK = 3
ROPE_THETA = 10000.0
TOPK_E = 4
SWIGLU_ALPHA = 1.702
SWIGLU_LIMIT = 7.0
LN_EPS = 1e-5
RMS_EPS = 1e-6
VMEM_LIMIT = 56 * 1024 * 1024
SAMPLE_ROWS = 16
MOE_GROUP = 1280
MOE_SUB = 256
MOE_TF = 256

NN = (((1,), (0,)), ((), ()))
NT = (((1,), (1,)), ((), ()))
TN = (((0,), (0,)), ((), ()))


def _cp(*sem):
    return pltpu.CompilerParams(dimension_semantics=sem, vmem_limit_bytes=VMEM_LIMIT)


def _tile(n, cap, unit):
    return max(t for t in range(unit, min(n, cap) + 1, unit) if n % t == 0)


def _mm(a, b, dims=NN):
    return lax.dot_general(a, b, dims, preferred_element_type=F32)


def _dot1(a, b, dims=NN):
    return _mm(a.astype(BF16), b.astype(BF16), dims)


def _split2(x):
    hi = x.astype(BF16)
    return hi, (x - hi.astype(F32)).astype(BF16)


def _dot3(a, b, dims=NN):
    ah, al = _split2(a)
    bh, bl = _split2(b)
    return _mm(ah, bh, dims) + (_mm(al, bh, dims) + _mm(ah, bl, dims))


def _dot_sel(sel, x, dims=NN):
    hi = x.astype(BF16)
    r = x - hi.astype(F32)
    mid = r.astype(BF16)
    lo = (r - mid.astype(F32)).astype(BF16)
    return _mm(sel, hi, dims) + (_mm(sel, mid, dims) + _mm(sel, lo, dims))


def _sigmoid(x):
    return 1.0 / (1.0 + jnp.exp(-x))


def _softplus(x):
    return jnp.maximum(x, 0.0) + jnp.log(1.0 + jnp.exp(-jnp.abs(x)))


def _layer_norm(y, g, b):
    mu = jnp.mean(y, axis=-1, keepdims=True)
    d = y - mu
    var = jnp.mean(d * d, axis=-1, keepdims=True)
    return d * lax.rsqrt(var + LN_EPS) * g + b


def _ada_kernel(c_ref, w_ref, b_ref, o_ref):
    c = c_ref[...]
    o_ref[0] = _dot3(c * _sigmoid(c), w_ref[0]) + b_ref[0]


def _ada_all(c_all, w_ada, b_ada, tn=1024):
    depth, d, n = w_ada.shape
    r = c_all.shape[0]
    return pl.pallas_call(
        _ada_kernel,
        grid=(depth, n // tn),
        in_specs=[pl.BlockSpec((r, d), lambda l, j: (0, 0)),
                  pl.BlockSpec((1, d, tn), lambda l, j: (l, 0, j)),
                  pl.BlockSpec((1, 1, tn), lambda l, j: (l, 0, j))],
        out_specs=pl.BlockSpec((1, r, tn), lambda l, j: (l, 0, j)),
        out_shape=jax.ShapeDtypeStruct((depth, r, n), F32),
        compiler_params=_cp("parallel", "parallel"),
        name="ada_mod",
    )(c_all, w_ada, b_ada.reshape(depth, 1, n))


def _inproj_kernel(x_ref, sh_ref, sc_ref, w_ref, ws_ref, o_ref, os_ref, h_ref):
    @pl.when(pl.program_id(1) == 0)
    def _():
        h = x_ref[...] * (1.0 + sc_ref[...]) + sh_ref[...]
        h_ref[...] = h.astype(BF16)
        os_ref[...] = _dot3(h, ws_ref[...])

    o_ref[...] = _mm(h_ref[...], w_ref[0])


def _inproj(x2d, mod3, w_main, w_small, layer, tm):
    m, d = x2d.shape
    n = w_main.shape[2]
    tn = _tile(n, 512, LANES)
    groups, r, _ = mod3.shape
    tiles_per_group = m // tm // groups
    mod_spec = lambda chunk: pl.BlockSpec((None, r, d), lambda i, j: (i // tiles_per_group, 0, chunk))
    return pl.pallas_call(
        _inproj_kernel,
        grid=(m // tm, n // tn),
        in_specs=[pl.BlockSpec((tm, d), lambda i, j: (i, 0)),
                  mod_spec(0), mod_spec(1),
                  pl.BlockSpec((1, d, tn), lambda i, j: (layer, 0, j)),
                  pl.BlockSpec((None, d, LANES), lambda i, j: (layer, 0, 0))],
        out_specs=[pl.BlockSpec((tm, tn), lambda i, j: (i, j)),
                   pl.BlockSpec((tm, LANES), lambda i, j: (i, 0))],
        out_shape=[jax.ShapeDtypeStruct((m, n), F32), jax.ShapeDtypeStruct((m, LANES), F32)],
        scratch_shapes=[pltpu.VMEM((tm, d), BF16)],
        compiler_params=_cp("parallel", "arbitrary"),
        name="in_proj",
    )(x2d, mod3, mod3, w_main, w_small)


def _gdn_kernel(qkv_ref, z_ref, sm_ref, cprev_ref, s0_ref, cw_ref, alog_ref, dtb_ref, nw_ref,
                o_ref, s_ref, ext_ref, *, n_heads, n_valid, precise):
    C = GDN_CHUNK
    W = n_heads * HEAD_DIM
    dotp = _dot3 if precise else _dot1

    @pl.when(pl.program_id(1) == 0)
    def _():
        ext_ref[0:8, :] = cprev_ref[0]
        s_ref[...] = s0_ref[...]

    ext_ref[8:8 + C, :] = qkv_ref[...]
    cw = cw_ref[...]
    conv = ext_ref[5:5 + C, :] * cw[0:1, :]
    for i in range(1, CONV_W):
        conv = conv + ext_ref[5 + i:5 + i + C, :] * cw[i:i + 1, :]
    u = conv * _sigmoid(conv)
    ext_ref[0:8, :] = ext_ref[C:C + 8, :]

    sm = sm_ref[...]
    beta_all = _sigmoid(sm)
    g_all = -jnp.exp(alog_ref[...]) * _softplus(sm + dtb_ref[...])
    if n_valid < C:
        valid = lax.broadcasted_iota(I32, (C, LANES), 0) < n_valid
        beta_all = jnp.where(valid, beta_all, 0.0)
        g_all = jnp.where(valid, g_all, 0.0)

    P2 = 2 * C
    ii = lax.broadcasted_iota(I32, (P2, P2), 0)
    jj = lax.broadcasted_iota(I32, (P2, P2), 1)
    same = (ii // C) == (jj // C)
    tri_incl = same & (ii >= jj)
    tri_strict = same & (ii > jj)
    tri_bf = jnp.where(tri_incl, 1.0, 0.0).astype(BF16)
    eye = jnp.where(ii == jj, 1.0, 0.0).astype(F32)
    lane0 = jnp.where(jj == 0, 1.0, 0.0).astype(BF16)
    pairs = [(2 * p, 2 * p + 1) for p in range(n_heads // 2)]
    rows = (slice(0, C), slice(C, P2))

    def stack(f, pr):
        return jnp.concatenate([f(pr[0]), f(pr[1])], axis=0)

    def l2n(x):
        return x * lax.rsqrt(jnp.sum(x * x, axis=-1, keepdims=True) + 1e-6)

    head = lambda base: (lambda h: u[:, base + h * HEAD_DIM:base + (h + 1) * HEAD_DIM])
    qn = [l2n(stack(head(0), pr)) * (HEAD_DIM ** -0.5) for pr in pairs]
    kn = [l2n(stack(head(W), pr)) for pr in pairs]
    vv = [stack(head(2 * W), pr) for pr in pairs]
    beta = [stack(lambda h: beta_all[:, h:h + 1], pr) for pr in pairs]
    gb = [stack(lambda h: jnp.broadcast_to(g_all[:, n_heads + h:n_heads + h + 1], (C, LANES)), pr) for pr in pairs]
    gc = [_dot_sel(tri_bf, x) for x in gb]
    grow = [_dot_sel(lane0, x, NT) for x in gc]
    decay = [jnp.where(tri_incl, jnp.exp(jnp.where(tri_incl, c_ - r_, 0.0)), 0.0) for c_, r_ in zip(gc, grow)]
    eg = [jnp.exp(x) for x in gc]
    kk = [dotp(x, x, NT) for x in kn]
    pw = [jnp.where(tri_strict, -(b_ * k_ * d_), 0.0) for b_, k_, d_ in zip(beta, kk, decay)]
    tinv = [eye + x for x in pw]
    for _ in range(C.bit_length() - 2):
        pw = [_dot3(x, x) for x in pw]
        tinv = [t_ + _dot3(t_, x) for t_, x in zip(tinv, pw)]
    sol = [_dot3(t_, jnp.concatenate([b_ * v_, b_ * e_ * k_], axis=1))
           for t_, b_, v_, e_, k_ in zip(tinv, beta, vv, eg, kn)]
    qk = [dotp(q_, k_, NT) * d_ for q_, k_, d_ in zip(qn, kn, decay)]
    qe = [q_ * e_ for q_, e_ in zip(qn, eg)]

    for p, pr in enumerate(pairs):
        s_old = [s_ref[0, h] for h in pr]
        un = [sol[p][rs, :HEAD_DIM] - dotp(sol[p][rs, HEAD_DIM:], s_) for rs, s_ in zip(rows, s_old)]
        un2 = jnp.concatenate(un, axis=0)
        o = jnp.concatenate([dotp(qe[p][rs, :], s_) for rs, s_ in zip(rows, s_old)], axis=0) + dotp(qk[p], un2)
        for h, rs, s_, un_ in zip(pr, rows, s_old, un):
            gch = gc[p][rs, :]
            glast = gch[C - 1:C, :]
            s_ref[0, h] = jnp.exp(glast) * s_ + dotp(kn[p][rs, :] * jnp.exp(glast - gch), un_, TN)
        on = o * lax.rsqrt(jnp.mean(o * o, axis=-1, keepdims=True) + RMS_EPS) * nw_ref[...]
        for h, rs in zip(pr, rows):
            sl = slice(h * HEAD_DIM, (h + 1) * HEAD_DIM)
            zh = z_ref[:, sl]
            o_ref[:, sl] = (on[rs, :] * (zh * _sigmoid(zh))).astype(o_ref.dtype)


def _gdn(proj, small, conv_prev8, s0, conv_w, alog_row, dtb_row, norm_w, n_heads, n_valid, precise):
    C = GDN_CHUNK
    W = n_heads * HEAD_DIM
    B = s0.shape[0]
    nc = proj.shape[0] // B // C
    kern = functools.partial(_gdn_kernel, n_heads=n_heads, n_valid=n_valid, precise=precise)
    full = lambda shape: pl.BlockSpec(shape, lambda b, c: (0,) * len(shape))
    return pl.pallas_call(
        kern,
        grid=(B, nc),
        in_specs=[pl.BlockSpec((C, 3 * W), lambda b, c: (b * nc + c, 0)),
                  pl.BlockSpec((C, W), lambda b, c: (b * nc + c, 3)),
                  pl.BlockSpec((C, LANES), lambda b, c: (b * nc + c, 0)),
                  pl.BlockSpec((1, 8, 3 * W), lambda b, c: (b, 0, 0)),
                  pl.BlockSpec((1, n_heads, HEAD_DIM, HEAD_DIM), lambda b, c: (b, 0, 0, 0)),
                  full((CONV_W, 3 * W)), full((1, LANES)), full((1, LANES)), full((1, HEAD_DIM))],
        out_specs=[pl.BlockSpec((C, W), lambda b, c: (b * nc + c, 0)),
                   pl.BlockSpec((1, n_heads, HEAD_DIM, HEAD_DIM), lambda b, c: (b, 0, 0, 0))],
        out_shape=[jax.ShapeDtypeStruct((B * nc * C, W), BF16),
                   jax.ShapeDtypeStruct((B, n_heads, HEAD_DIM, HEAD_DIM), F32)],
        scratch_shapes=[pltpu.VMEM((C + 8, 3 * W), F32)],
        compiler_params=_cp("parallel", "arbitrary"),
        name="gdn",
    )(proj, proj, small, conv_prev8, s0, conv_w, alog_row, dtb_row, norm_w)


def _rope(x, cos, sin):
    return x * cos + pltpu.roll(x, HEAD_DIM // 2, 1) * sin


def _rope_kernel(q_ref, k_ref, cos_ref, sin_ref, qo_ref, ko_ref, km_ref, *, n_heads):
    cos = cos_ref[...]
    sin = sin_ref[...]
    for h in range(n_heads):
        sl = slice(h * HEAD_DIM, (h + 1) * HEAD_DIM)
        qo_ref[:, sl] = _rope(q_ref[:, sl], cos, sin) * (HEAD_DIM ** -0.5)
        kr = _rope(k_ref[:, sl], cos, sin)
        ko_ref[:, sl] = kr
        km_ref[0, :, sl] = jnp.sum(kr, axis=0, keepdims=True) * (1.0 / MOBA_BLOCK)


def _rope_prompt(proj, cos, sin, n_heads, B):
    W = n_heads * HEAD_DIM
    m = proj.shape[0]
    nb = m // B // MOBA_BLOCK
    return pl.pallas_call(
        functools.partial(_rope_kernel, n_heads=n_heads),
        grid=(B, nb),
        in_specs=[pl.BlockSpec((MOBA_BLOCK, W), lambda b, i: (b * nb + i, 4)),
                  pl.BlockSpec((MOBA_BLOCK, W), lambda b, i: (b * nb + i, 5)),
                  pl.BlockSpec((MOBA_BLOCK, HEAD_DIM), lambda b, i: (i, 0)),
                  pl.BlockSpec((MOBA_BLOCK, HEAD_DIM), lambda b, i: (i, 0))],
        out_specs=[pl.BlockSpec((MOBA_BLOCK, W), lambda b, i: (b * nb + i, 0)),
                   pl.BlockSpec((MOBA_BLOCK, W), lambda b, i: (b * nb + i, 0)),
                   pl.BlockSpec((1, 1, W), lambda b, i: (b * nb + i, 0, 0))],
        out_shape=[jax.ShapeDtypeStruct((m, W), F32), jax.ShapeDtypeStruct((m, W), F32),
                   jax.ShapeDtypeStruct((B * nb, 1, W), F32)],
        compiler_params=_cp("parallel", "parallel"),
        name="rope_prompt",
    )(proj, proj, cos, sin)


def _moba_kernel(q_ref, k_ref, v_ref, km_ref, o_ref, *, nb):
    i = pl.program_id(2)
    tq = MOBA_BLOCK
    q = q_ref[...]
    km = jnp.concatenate([km_ref[0], jnp.zeros((LANES - nb, HEAD_DIM), F32)], axis=0)
    s = _dot3(q, km, NT)
    lane = lax.broadcasted_iota(I32, (tq, LANES), 1)
    past = lane < i
    s = jnp.where(past, s, -jnp.inf)
    rank = jnp.zeros((tq, LANES), I32)
    for j in range(nb):
        sj = s[:, j:j + 1]
        rank = rank + jnp.where((sj > s) | ((sj == s) & (j < lane)), 1, 0)
    sel = jnp.where(past & (rank < MOBA_TOPK), 1.0, 0.0)
    qb = q.astype(BF16)

    def block(j):
        rs = pl.ds(pl.multiple_of(j * tq, tq), tq)
        return _mm(qb, k_ref[rs, :].astype(BF16), NT), v_ref[rs, :].astype(BF16)

    lg, vb = block(i)
    lg = jnp.where(lax.broadcasted_iota(I32, (tq, tq), 1) <= lax.broadcasted_iota(I32, (tq, tq), 0), lg, -jnp.inf)
    mx = jnp.max(lg, axis=-1, keepdims=True)
    p = jnp.exp(lg - mx)
    carry = (mx, jnp.sum(p, axis=-1, keepdims=True), _mm(p.astype(BF16), vb))

    def body(j, carry):
        mx, den, acc = carry
        lg, vb = block(j)
        chosen = jnp.sum(jnp.where(lane == j, sel, 0.0), axis=-1, keepdims=True) > 0.5
        lg = jnp.where(chosen, lg, -jnp.inf)
        mx_new = jnp.maximum(mx, jnp.max(lg, axis=-1, keepdims=True))
        scale = jnp.exp(mx - mx_new)
        p = jnp.exp(lg - mx_new)
        return mx_new, scale * den + jnp.sum(p, axis=-1, keepdims=True), scale * acc + _mm(p.astype(BF16), vb)

    mx, den, acc = lax.fori_loop(0, i, body, carry)
    o_ref[...] = (acc / den).astype(o_ref.dtype)


def _moba_prompt(q_rot, k_rot, proj, kmean, n_heads, B):
    m, W = q_rot.shape
    S = m // B
    nb = S // MOBA_BLOCK
    return pl.pallas_call(
        functools.partial(_moba_kernel, nb=nb),
        grid=(B, n_heads, nb),
        in_specs=[pl.BlockSpec((MOBA_BLOCK, HEAD_DIM), lambda b, h, i: (b * nb + i, h)),
                  pl.BlockSpec((S, HEAD_DIM), lambda b, h, i: (b, h)),
                  pl.BlockSpec((S, HEAD_DIM), lambda b, h, i: (b, 6 * n_heads + h)),
                  pl.BlockSpec((1, nb, HEAD_DIM), lambda b, h, i: (b, 0, h))],
        out_specs=pl.BlockSpec((MOBA_BLOCK, HEAD_DIM), lambda b, h, i: (b * nb + i, h)),
        out_shape=jax.ShapeDtypeStruct((m, W), BF16),
        compiler_params=_cp("parallel", "parallel", "arbitrary"),
        name="moba_prompt",
    )(q_rot, k_rot, proj, kmean.reshape(B, nb, W))


def _rope_s_kernel(q_ref, k_ref, cos_ref, sin_ref, qo_ref, ko_ref, *, n_heads):
    cos = cos_ref[...]
    sin = sin_ref[...]
    for h in range(n_heads):
        sl = slice(h * HEAD_DIM, (h + 1) * HEAD_DIM)
        qo_ref[:, sl] = _rope(q_ref[:, sl], cos, sin) * (HEAD_DIM ** -0.5)
        ko_ref[:, sl] = _rope(k_ref[:, sl], cos, sin)


def _rope_sample(proj_s, cos, sin, n_heads):
    W = n_heads * HEAD_DIM
    r = proj_s.shape[0]
    return pl.pallas_call(
        functools.partial(_rope_s_kernel, n_heads=n_heads),
        grid=(1,),
        in_specs=[pl.BlockSpec((r, W), lambda i: (0, 4)), pl.BlockSpec((r, W), lambda i: (0, 5)),
                  pl.BlockSpec((1, HEAD_DIM), lambda i: (0, 0)), pl.BlockSpec((1, HEAD_DIM), lambda i: (0, 0))],
        out_specs=[pl.BlockSpec((r, W), lambda i: (0, 0)), pl.BlockSpec((r, W), lambda i: (0, 0))],
        out_shape=[jax.ShapeDtypeStruct((r, W), F32), jax.ShapeDtypeStruct((r, W), F32)],
        compiler_params=_cp("arbitrary"),
        name="rope_sample",
    )(proj_s, proj_s, cos, sin)


def _pool_mean_kernel(pt_ref, *refs, pages_per_step, pages_per_block):
    o_ref = refs[pages_per_step]
    for n in range(pages_per_step // pages_per_block):
        acc = jnp.sum(refs[n * pages_per_block][0, 0], axis=0)
        for t in range(1, pages_per_block):
            acc = acc + jnp.sum(refs[n * pages_per_block + t][0, 0], axis=0)
        o_ref[0, n] = acc * (1.0 / MOBA_BLOCK)


def _pool_block_means(pool_k, page_table, layer, pages_per_step=16):
    _, _, page, H, _ = pool_k.shape
    B, n_pages = page_table.shape
    ppb = MOBA_BLOCK // page
    bps = pages_per_step // ppb
    nblk = n_pages // ppb
    page_spec = lambda t: pl.BlockSpec(
        (1, 1, page, H, HEAD_DIM), lambda b, s, pt: (layer, pt[b, s * pages_per_step + t], 0, 0, 0))
    return pl.pallas_call(
        functools.partial(_pool_mean_kernel, pages_per_step=pages_per_step, pages_per_block=ppb),
        grid_spec=pltpu.PrefetchScalarGridSpec(
            num_scalar_prefetch=1,
            grid=(B, n_pages // pages_per_step),
            in_specs=[page_spec(t) for t in range(pages_per_step)],
            out_specs=pl.BlockSpec((1, bps, H, HEAD_DIM), lambda b, s, pt: (b * (nblk // bps) + s, 0, 0, 0)),
        ),
        out_shape=jax.ShapeDtypeStruct((B * nblk // bps, bps, H, HEAD_DIM), F32),
        compiler_params=_cp("parallel", "arbitrary"),
        name="pool_block_means",
    )(page_table, *([pool_k] * pages_per_step)).reshape(B, nblk, H * HEAD_DIM)


def _sample_select_kernel(q_ref, km_ref, sel_ref, *, n_heads, nblk):
    q = q_ref[0]
    rowi = lax.broadcasted_iota(I32, (n_heads, LANES), 0)
    lane = lax.broadcasted_iota(I32, (n_heads, LANES), 1)
    s = jnp.full((n_heads, LANES), -jnp.inf, F32)
    for h in range(n_heads):
        kmh = jnp.concatenate([km_ref[0, :, h * HEAD_DIM:(h + 1) * HEAD_DIM],
                               jnp.zeros((LANES - nblk, HEAD_DIM), F32)], axis=0)
        s = jnp.where((rowi == h) & (lane < nblk), _dot3(q, kmh, NT), s)
    out = jnp.zeros((n_heads, LANES), I32)
    for t in range(MOBA_TOPK):
        mx = jnp.max(s, axis=-1, keepdims=True)
        idx = jnp.min(jnp.where(s == mx, lane, LANES), axis=-1, keepdims=True)
        out = jnp.where(lane == t, idx, out)
        s = jnp.where(lane == idx, -jnp.inf, s)
    sel_ref[0] = out


def _sample_select(q3, kmean_s):
    B, n_heads, _ = q3.shape
    nblk, W = kmean_s.shape[1:]
    return pl.pallas_call(
        functools.partial(_sample_select_kernel, n_heads=n_heads, nblk=nblk),
        grid=(B,),
        in_specs=[pl.BlockSpec((1, n_heads, HEAD_DIM), lambda b: (b, 0, 0)),
                  pl.BlockSpec((1, nblk, W), lambda b: (b, 0, 0))],
        out_specs=pl.BlockSpec((1, n_heads, LANES), lambda b: (b, 0, 0)),
        out_shape=jax.ShapeDtypeStruct((B, n_heads, LANES), I32),
        compiler_params=_cp("parallel"),
        name="sample_select",
    )(q3, kmean_s)


def _sample_attn_kernel(pg_ref, q_ref, kn_ref, vn_ref, *refs, n_pages):
    kp_refs, vp_refs, o_ref = refs[:n_pages], refs[n_pages:2 * n_pages], refs[2 * n_pages]
    h = pl.program_id(1)
    page = kp_refs[0].shape[2]
    q = q_ref[0, pl.ds(h, 1), :]
    q8 = jnp.broadcast_to(q, (8, HEAD_DIM))
    head = lambda ref: ref[0, 0, :, pl.ds(h, 1), :].reshape(page, HEAD_DIM)
    s_new = jnp.sum(q * kn_ref[0, pl.ds(h, 1), :], axis=-1, keepdims=True)
    s = [_dot3(q8, head(r), NT)[0:1, :] for r in kp_refs]
    mx = s_new
    for x in s:
        mx = jnp.maximum(mx, jnp.max(x, axis=-1, keepdims=True))
    p_new = jnp.exp(s_new - mx)
    den = p_new
    acc = p_new * vn_ref[0, pl.ds(h, 1), :]
    for x, r in zip(s, vp_refs):
        p = jnp.exp(x - mx)
        den = den + jnp.sum(p, axis=-1, keepdims=True)
        acc = acc + _dot3(jnp.broadcast_to(p, (8, page)), head(r))[0:1, :]
    o_ref[0, pl.ds(h, 1), :] = acc / den


def _sample_attn(pages_flat, q3, pool_k, pool_v, kn3, vn3, layer, n_pages):
    B, n_heads, _ = q3.shape
    page = pool_k.shape[2]
    pool_spec = lambda t: pl.BlockSpec(
        (1, 1, page, n_heads, HEAD_DIM), lambda b, h, pg: (layer, pg[(b * n_heads + h) * n_pages + t], 0, 0, 0))
    tok_spec = pl.BlockSpec((1, n_heads, HEAD_DIM), lambda b, h, pg: (b, 0, 0))
    pool_specs = [pool_spec(t) for t in range(n_pages)]
    return pl.pallas_call(
        functools.partial(_sample_attn_kernel, n_pages=n_pages),
        grid_spec=pltpu.PrefetchScalarGridSpec(
            num_scalar_prefetch=1,
            grid=(B, n_heads),
            in_specs=[tok_spec, tok_spec, tok_spec] + pool_specs + pool_specs,
            out_specs=tok_spec,
        ),
        out_shape=jax.ShapeDtypeStruct((B, n_heads, HEAD_DIM), F32),
        compiler_params=_cp("parallel", "arbitrary"),
        name="sample_attn",
    )(pages_flat, q3, kn3, vn3, *([pool_k] * n_pages), *([pool_v] * n_pages))


def _oproj_kernel(oa_ref, ob_ref, wa_ref, wb_ref, x_ref, gt_ref, sc_ref, sh_ref, g_ref, b_ref, wr_ref, br_ref,
                  x1_ref, h2_ref, lg_ref, *, alpha):
    mix = _mm(oa_ref[...], wa_ref[0]) + _mm(ob_ref[...], wb_ref[0])
    x1 = _layer_norm(alpha * x_ref[...] + (1.0 + gt_ref[...]) * mix, g_ref[0], b_ref[0])
    x1_ref[...] = x1
    h2 = x1 * (1.0 + sc_ref[...]) + sh_ref[...]
    h2_ref[...] = h2
    lg_ref[...] = _dot3(h2, wr_ref[0]) + br_ref[0]


def _oproj(o_a, o_b, w_o_bf, x2d, mod3, ln_g, ln_b, w_router, b_router, layer, tm, alpha):
    m, d = x2d.shape
    W = o_a.shape[1]
    groups, r, _ = mod3.shape
    tiles_per_group = m // tm // groups
    mod_spec = lambda chunk: pl.BlockSpec((None, r, d), lambda i: (i // tiles_per_group, 0, chunk))
    lyr = lambda shape: pl.BlockSpec(shape, lambda i: (layer,) + (0,) * (len(shape) - 1))
    return pl.pallas_call(
        functools.partial(_oproj_kernel, alpha=alpha),
        grid=(m // tm,),
        in_specs=[pl.BlockSpec((tm, W), lambda i: (i, 0)), pl.BlockSpec((tm, W), lambda i: (i, 0)),
                  pl.BlockSpec((1, W, d), lambda i: (layer, 0, 0)), pl.BlockSpec((1, W, d), lambda i: (layer, 1, 0)),
                  pl.BlockSpec((tm, d), lambda i: (i, 0)),
                  mod_spec(2), mod_spec(4), mod_spec(3),
                  lyr((1, 1, d)), lyr((1, 1, d)), lyr((1, d, LANES)), lyr((1, 1, LANES))],
        out_specs=[pl.BlockSpec((tm, d), lambda i: (i, 0)), pl.BlockSpec((tm, d), lambda i: (i, 0)),
                   pl.BlockSpec((tm, LANES), lambda i: (i, 0))],
        out_shape=[jax.ShapeDtypeStruct((m, d), F32), jax.ShapeDtypeStruct((m, d), F32),
                   jax.ShapeDtypeStruct((m, LANES), F32)],
        compiler_params=_cp("parallel"),
        name="out_proj",
    )(o_a, o_b, w_o_bf, w_o_bf, x2d, mod3, mod3, mod3, ln_g, ln_b, w_router, b_router)


def _route_kernel(lg_ref, e_ref, g_ref, *, n_exp):
    shape = lg_ref.shape
    lane = lax.broadcasted_iota(I32, shape, 1)
    l = jnp.where(lane < n_exp, lg_ref[...], -jnp.inf)
    vals, idxs = [], []
    for _ in range(TOPK_E):
        mx = jnp.max(l, axis=-1, keepdims=True)
        idx = jnp.min(jnp.where(l == mx, lane, LANES), axis=-1, keepdims=True)
        vals.append(mx)
        idxs.append(idx)
        l = jnp.where(lane == idx, -jnp.inf, l)
    ex = [jnp.exp(v - vals[0]) for v in vals]
    tot = ex[0]
    for e in ex[1:]:
        tot = tot + e
    e_out = jnp.zeros(shape, I32)
    g_out = jnp.zeros(shape, F32)
    for k in range(TOPK_E):
        e_out = jnp.where(lane == k, idxs[k], e_out)
        g_out = jnp.where(lane == k, ex[k] / tot, g_out)
    e_ref[...] = e_out
    g_ref[...] = g_out


def _route(logits, n_exp, tm):
    m = logits.shape[0]
    spec = pl.BlockSpec((tm, LANES), lambda i: (i, 0))
    return pl.pallas_call(
        functools.partial(_route_kernel, n_exp=n_exp),
        grid=(m // tm,),
        in_specs=[spec], out_specs=[spec, spec],
        out_shape=[jax.ShapeDtypeStruct((m, LANES), I32), jax.ShapeDtypeStruct((m, LANES), F32)],
        compiler_params=_cp("parallel"),
        name="route_top4",
    )(logits)


def _rows_copy(src_hbm, idx, dst_ref, row, n, sem):
    return pltpu.make_async_copy(src_hbm.at[pl.ds(idx, n), :], dst_ref.at[pl.ds(row, n), :], sem)


def _moe_kernel(ge_ref, gr_ref, tok_ref, x_hbm, wg_ref, wu_ref, bg_ref, bu_ref, wd_ref, bd_ref, y_ref,
                xf_ref, xb_ref, sem):
    g = pl.program_id(0)
    j = pl.program_id(1)
    rows = gr_ref[g]
    half = MOE_SUB // 2
    n_half = (rows + half - 1) // half

    @pl.when((rows > 0) & (j == 0))
    def _():
        def start(r, c):
            _rows_copy(x_hbm, tok_ref[0, 0, r], xf_ref, r, 1, sem.at[0]).start()
            return c

        lax.fori_loop(0, n_half * half, start, 0)

        def finish(s, c):
            _rows_copy(x_hbm, 0, xf_ref, pl.multiple_of(s * half, half), half, sem.at[0]).wait()
            return c

        lax.fori_loop(0, n_half, finish, 0)

        def cast(s, c):
            rs = pl.ds(pl.multiple_of(s * half, half), half)
            xb_ref[rs, :] = xf_ref[rs, :].astype(BF16)
            return c

        lax.fori_loop(0, n_half, cast, 0)

    @pl.when((rows == 0) & (j == 0))
    def _():
        y_ref[...] = jnp.zeros(y_ref.shape, F32)

    @pl.when(rows > 0)
    def _():
        wg = wg_ref[0, 0].astype(BF16)
        wu = wu_ref[0, 0].astype(BF16)
        wd = wd_ref[0, 0].astype(BF16)

        def run(r0, nr):
            rs = slice(r0, r0 + nr)
            xb = xb_ref[rs, :]
            gate = jnp.minimum(_mm(xb, wg) + bg_ref[0, 0], SWIGLU_LIMIT)
            up = jnp.clip(_mm(xb, wu) + bu_ref[0, 0], -SWIGLU_LIMIT, SWIGLU_LIMIT)
            act = (up + 1.0) * gate * _sigmoid(SWIGLU_ALPHA * gate)
            part = _mm(act.astype(BF16), wd)

            @pl.when(j == 0)
            def _():
                y_ref[rs, :] = part + bd_ref[0, 0]

            @pl.when(j > 0)
            def _():
                y_ref[rs, :] += part

        def clear(r0, nr):
            y_ref[r0:r0 + nr, :] = jnp.zeros((nr, y_ref.shape[1]), F32)

        for r in range(MOE_GROUP // MOE_SUB):
            r0 = r * MOE_SUB
            left = rows - r0
            pl.when(left > half)(functools.partial(run, r0, MOE_SUB))
            pl.when((left > 0) & (left <= half))(functools.partial(run, r0, half))
            pl.when((j == 0) & (left > 0) & (left <= half))(functools.partial(clear, r0 + half, half))
            pl.when((j == 0) & (left <= 0))(functools.partial(clear, r0, MOE_SUB))


def _moe_experts(group_e, group_rows, slot_tok, x_rows, w_gu, b_gu4, w_dn, b_dn4, layer):
    n_groups = group_e.shape[0]
    d = x_rows.shape[1]
    f = w_dn.shape[2]
    nf = f // MOE_TF
    jx = lambda j, gr, g: jnp.where(gr[g] > 0, j, nf - 1)
    return pl.pallas_call(
        _moe_kernel,
        grid_spec=pltpu.PrefetchScalarGridSpec(
            num_scalar_prefetch=2,
            grid=(n_groups, nf),
            in_specs=[
                pl.BlockSpec((1, 1, MOE_GROUP), lambda g, j, ge, gr: (g, 0, 0), memory_space=pltpu.SMEM),
                pl.BlockSpec(memory_space=pl.ANY),
                pl.BlockSpec((1, 1, d, MOE_TF), lambda g, j, ge, gr: (layer, ge[g], 0, jx(j, gr, g))),
                pl.BlockSpec((1, 1, d, MOE_TF), lambda g, j, ge, gr: (layer, ge[g], 0, nf + jx(j, gr, g))),
                pl.BlockSpec((1, 1, 1, MOE_TF), lambda g, j, ge, gr: (layer, ge[g], 0, jx(j, gr, g))),
                pl.BlockSpec((1, 1, 1, MOE_TF), lambda g, j, ge, gr: (layer, ge[g], 0, nf + jx(j, gr, g))),
                pl.BlockSpec((1, 1, MOE_TF, d), lambda g, j, ge, gr: (layer, ge[g], jx(j, gr, g), 0)),
                pl.BlockSpec((1, 1, 1, d), lambda g, j, ge, gr: (layer, ge[g], 0, 0)),
            ],
            out_specs=pl.BlockSpec((MOE_GROUP, d), lambda g, j, ge, gr: (g, 0)),
            scratch_shapes=[pltpu.VMEM((MOE_GROUP, d), F32), pltpu.VMEM((MOE_GROUP, d), BF16),
                            pltpu.SemaphoreType.DMA((1,))],
        ),
        out_shape=jax.ShapeDtypeStruct((n_groups * MOE_GROUP, d), F32),
        compiler_params=_cp("arbitrary", "arbitrary"),
        name="moe_experts",
    )(group_e, group_rows, slot_tok, x_rows, w_gu, w_gu, b_gu4, b_gu4, w_dn, b_dn4)


def _dispatch(top_e, n_exp):
    m = top_e.shape[0]
    a = m * TOPK_E
    R = MOE_GROUP
    n_groups = -(-a // R) + n_exp
    e_flat = top_e.reshape(a)
    onehot = (e_flat[:, None] == jnp.arange(n_exp, dtype=I32)[None, :]).astype(I32)
    csum = jnp.cumsum(onehot, axis=0)
    rank = jnp.take_along_axis(csum, e_flat[:, None], axis=1)[:, 0] - 1
    counts = csum[-1]
    ng_e = (counts + R - 1) // R
    g_end = jnp.cumsum(ng_e)
    g_start = g_end - ng_e
    dest = g_start[e_flat] * R + rank
    gid = jnp.arange(n_groups, dtype=I32)
    used = gid < g_end[-1]
    last = jnp.maximum(g_end[-1] - 1, 0)
    ge = jnp.minimum(jnp.sum((g_end[None, :] <= jnp.minimum(gid, last)[:, None]).astype(I32), axis=1), n_exp - 1)
    rows = jnp.where(used, jnp.clip(counts[ge] - (gid - g_start[ge]) * R, 0, R), 0).astype(I32)
    tok = jnp.arange(a, dtype=I32) // TOPK_E
    slot_tok = jnp.full((n_groups * R,), m, I32).at[dest].set(tok)
    return dest, slot_tok.reshape(n_groups, 1, R), ge, rows


def _final_kernel(dest_ref, y_hbm, x1_ref, gate_ref, gt_ref, g_ref, b_ref, o_ref, ybuf, sem, *, alpha):
    tm = x1_ref.shape[0]

    def start(r, c):
        for k in range(TOPK_E):
            _rows_copy(y_hbm, dest_ref[0, 0, r * TOPK_E + k], ybuf.at[k], r, 1, sem.at[0]).start()
        return c

    lax.fori_loop(0, tm, start, 0)
    for k in range(TOPK_E):
        _rows_copy(y_hbm, 0, ybuf.at[k], 0, tm, sem.at[0]).wait()
    gates = gate_ref[...]
    moe = ybuf[0] * gates[:, 0:1]
    for k in range(1, TOPK_E):
        moe = moe + ybuf[k] * gates[:, k:k + 1]
    o_ref[...] = _layer_norm(alpha * x1_ref[...] + (1.0 + gt_ref[...]) * moe, g_ref[0], b_ref[0])


def _final(dest2d, y, x1, gates, mod3, ln_g, ln_b, layer, tm, alpha):
    m, d = x1.shape
    groups, r, _ = mod3.shape
    tiles_per_group = m // tm // groups
    row = pl.BlockSpec((tm, d), lambda i: (i, 0))
    lyr = pl.BlockSpec((1, 1, d), lambda i: (layer, 0, 0))
    return pl.pallas_call(
        functools.partial(_final_kernel, alpha=alpha),
        grid=(m // tm,),
        in_specs=[pl.BlockSpec((1, 1, TOPK_E * tm), lambda i: (i, 0, 0), memory_space=pltpu.SMEM),
                  pl.BlockSpec(memory_space=pl.ANY),
                  row, pl.BlockSpec((tm, LANES), lambda i: (i, 0)),
                  pl.BlockSpec((None, r, d), lambda i: (i // tiles_per_group, 0, 5)), lyr, lyr],
        out_specs=row,
        out_shape=jax.ShapeDtypeStruct((m, d), F32),
        scratch_shapes=[pltpu.VMEM((TOPK_E, tm, d), F32), pltpu.SemaphoreType.DMA((1,))],
        compiler_params=_cp("arbitrary"),
        name="final_ln",
    )(dest2d, y, x1, gates, mod3, ln_g, ln_b)


def _rope_tables(pos):
    half = HEAD_DIM // 2
    inv = ROPE_THETA ** (-2.0 * jnp.arange(half, dtype=F32) / HEAD_DIM)
    ang = pos.astype(F32)[:, None] * inv[None, :]
    cos, sin = jnp.cos(ang), jnp.sin(ang)
    return jnp.concatenate([cos, cos], -1), jnp.concatenate([-sin, sin], -1)


def _lane_row(v, offset):
    return jnp.zeros((1, LANES), F32).at[0, offset:offset + v.shape[0]].set(v)


def kernel(x_prompt, x_sample, cache_k, cache_v, state_gdn, state_conv, page_table, c_prompt, c_sample,
           w_in, conv_w, a_log, dt_bias, gdn_norm_w, w_o, w_ada, b_ada, ln1_g, ln1_b, ln2_g, ln2_b,
           w_router, b_router, w_gate_up, b_gate_up, w_down, b_down):
    B, S, D = x_prompt.shape
    Bs, Ss, _ = x_sample.shape
    depth = w_in.shape[0]
    H = a_log.shape[1]
    W = H * HEAD_DIM
    n_exp = w_router.shape[2]
    page = cache_k.shape[2]
    n_pages = page_table.shape[1]
    past = n_pages * page
    RS = SAMPLE_ROWS
    assert Ss == 1 and Bs <= 8 and S % MOBA_BLOCK == 0 and S >= CONV_W - 1
    assert past % MOBA_BLOCK == 0 and past // MOBA_BLOCK >= MOBA_TOPK and MOBA_BLOCK % page == 0
    assert n_exp <= LANES and 2 * H <= LANES and H % 2 == 0 and cache_k.shape[3] == H and w_in.shape[2] == 7 * W + 2 * H
    alpha = (2 * depth) ** 0.25
    MP = B * S

    w_main = jnp.concatenate([w_in[:, :, :4 * W], w_in[:, :, 4 * W + 2 * H:]], axis=-1).astype(BF16)
    w_small = jnp.pad(w_in[:, :, 4 * W:4 * W + 2 * H], ((0, 0), (0, 0), (0, LANES - 2 * H)))
    w_o_bf = w_o.astype(BF16)
    w_router_p = jnp.pad(w_router, ((0, 0), (0, 0), (0, LANES - n_exp)))
    b_router_p = jnp.pad(b_router, ((0, 0), (0, LANES - n_exp))).reshape(depth, 1, LANES)
    ln1_g3, ln1_b3 = ln1_g.reshape(depth, 1, D), ln1_b.reshape(depth, 1, D)
    ln2_g3, ln2_b3 = ln2_g.reshape(depth, 1, D), ln2_b.reshape(depth, 1, D)
    b_gu4 = b_gate_up.reshape(depth, n_exp, 1, -1)
    b_dn4 = b_down.reshape(depth, n_exp, 1, D)
    cos_p, sin_p = _rope_tables(jnp.arange(S, dtype=I32))
    cos_s, sin_s = _rope_tables(jnp.full((1,), past, I32))

    c_all = jnp.concatenate([c_prompt, jnp.zeros((8 - B % 8 if B % 8 else 0, D), F32),
                             c_sample, jnp.zeros((RS - Bs, D), F32)], axis=0)
    s_row0 = c_all.shape[0] - RS
    mod_all = _ada_all(c_all, w_ada, b_ada)

    xp = x_prompt.reshape(MP, D)
    xs = jnp.pad(x_sample.reshape(Bs, D), ((0, RS - Bs), (0, 0)))
    zero_conv = jnp.zeros((B, 8, 3 * W), F32)
    zero_state = jnp.zeros((B, H, HEAD_DIM, HEAD_DIM), F32)
    outs = [[] for _ in range(8)]
    ppb = MOBA_BLOCK // page

    for l in range(depth):
        mod_p = mod_all[l, :B].reshape(B, 1, 6 * D)
        mod_s = mod_all[l, s_row0:].reshape(1, RS, 6 * D)
        alog_row = _lane_row(a_log[l], H)
        dtb_row = _lane_row(dt_bias[l], H)
        nw = gdn_norm_w[l].reshape(1, HEAD_DIM)

        proj_p, small_p = _inproj(xp, mod_p, w_main, w_small, l, tm=_tile(S, 1024, 8))
        oa_p, gdn_p = _gdn(proj_p, small_p, zero_conv, zero_state, conv_w[l], alog_row, dtb_row, nw,
                           H, GDN_CHUNK, False)
        q_rot, k_rot, kmean = _rope_prompt(proj_p, cos_p, sin_p, H, B)
        ob_p = _moba_prompt(q_rot, k_rot, proj_p, kmean, H, B)

        proj_s, small_s = _inproj(xs, mod_s, w_main, w_small, l, tm=RS)
        pad_rows = lambda t: jnp.pad(t[:Bs, None, :], ((0, 0), (0, GDN_CHUNK - 1), (0, 0))).reshape(Bs * GDN_CHUNK, -1)
        conv_prev8 = jnp.pad(state_conv[l], ((0, 0), (8 - (CONV_W - 1), 0), (0, 0)))
        oa_s64, gdn_s = _gdn(pad_rows(proj_s), pad_rows(small_s), conv_prev8, state_gdn[l], conv_w[l],
                             alog_row, dtb_row, nw, H, 1, True)
        oa_s = jnp.pad(oa_s64[::GDN_CHUNK], ((0, RS - Bs), (0, 0)))
        qs_rot, ks_rot = _rope_sample(proj_s, cos_s, sin_s, H)
        q3 = qs_rot[:Bs].reshape(Bs, H, HEAD_DIM)
        kn3 = ks_rot[:Bs].reshape(Bs, H, HEAD_DIM)
        vn3 = proj_s[:Bs, 6 * W:7 * W].reshape(Bs, H, HEAD_DIM)
        kmean_s = _pool_block_means(cache_k, page_table, l)
        sel = _sample_select(q3, kmean_s)[:, :, :MOBA_TOPK]
        lpage = sel[..., None] * ppb + jnp.arange(ppb, dtype=I32)
        pages = jnp.take_along_axis(page_table, lpage.reshape(Bs, -1), axis=1)
        ob_s3 = _sample_attn(pages.reshape(-1).astype(I32), q3, cache_k, cache_v, kn3, vn3, l, MOBA_TOPK * ppb)
        ob_s = jnp.pad(ob_s3.reshape(Bs, W), ((0, RS - Bs), (0, 0))).astype(BF16)

        x1_p, h2_p, lg_p = _oproj(oa_p, ob_p, w_o_bf, xp, mod_p, ln1_g3, ln1_b3, w_router_p, b_router_p, l, 256, alpha)
        x1_s, h2_s, lg_s = _oproj(oa_s, ob_s, w_o_bf, xs, mod_s, ln1_g3, ln1_b3, w_router_p, b_router_p, l, RS, alpha)

        lg_all = jnp.concatenate([lg_p, lg_s], axis=0)
        m_all = MP + RS
        top_e, gates = _route(lg_all, n_exp, _tile(m_all, 1024, 8))
        dest, slot_tok, ge, rows = _dispatch(top_e[:, :TOPK_E], n_exp)
        h2_all = jnp.concatenate([h2_p, h2_s, jnp.zeros((8, D), F32)], axis=0)
        y = _moe_experts(ge, rows, slot_tok, h2_all, w_gate_up, b_gu4, w_down, b_dn4, l)

        tf = _tile(S, 256, 8)
        xp = _final(dest[:MP * TOPK_E].reshape(MP // tf, 1, TOPK_E * tf), y, x1_p, gates[:MP], mod_p,
                    ln2_g3, ln2_b3, l, tf, alpha)
        xs = _final(dest[MP * TOPK_E:].reshape(1, 1, TOPK_E * RS), y, x1_s, gates[MP:], mod_s,
                    ln2_g3, ln2_b3, l, RS, alpha)

        outs[0].append(k_rot.reshape(B, S, H, HEAD_DIM))
        outs[1].append(proj_p[:, 6 * W:7 * W].reshape(B, S, H, HEAD_DIM))
        outs[2].append(kn3.reshape(Bs, 1, H, HEAD_DIM))
        outs[3].append(vn3.reshape(Bs, 1, H, HEAD_DIM))
        outs[4].append(gdn_p)
        outs[5].append(gdn_s)
        outs[6].append(proj_p.reshape(B, S, -1)[:, S - (CONV_W - 1):, :3 * W])
        outs[7].append(jnp.concatenate([state_conv[l][:, 1:], proj_s[:Bs, None, :3 * W]], axis=1))

    return (xp.reshape(B, S, D), xs[:Bs].reshape(Bs, 1, D)) + tuple(jnp.stack(o) for o in outs)
```

```python
import functools

import jax
import jax.numpy as jnp
from jax import lax
from jax.experimental import pallas as pl
from jax.experimental.pallas import tpu as pltpu

F32 = jnp.float32
BF16 = jnp.bfloat16
I32 = jnp.int32

LANES = 128
HEAD_DIM = 128
CONV_W = 4
GDN_CHUNK = 64
MOBA_BLOCK = 256
MOBA_TOPK = 3
ROPE_THETA = 10000.0
TOPK_E = 4
SWIGLU_ALPHA = 1.702
SWIGLU_LIMIT = 7.0
LN_EPS = 1e-5
RMS_EPS = 1e-6
VMEM_LIMIT = 56 * 1024 * 1024
SAMPLE_ROWS = 16
MOE_GROUP = 1280
MOE_SUB = 256
MOE_TF = 256

NN = (((1,), (0,)), ((), ()))
NT = (((1,), (1,)), ((), ()))
TN = (((0,), (0,)), ((), ()))


def _cp(*sem):
    return pltpu.CompilerParams(dimension_semantics=sem, vmem_limit_bytes=VMEM_LIMIT)


def _tile(n, cap, unit):
    return max(t for t in range(unit, min(n, cap) + 1, unit) if n % t == 0)


def _mm(a, b, dims=NN):
    return lax.dot_general(a, b, dims, preferred_element_type=F32)


def _dot1(a, b, dims=NN):
    return _mm(a.astype(BF16), b.astype(BF16), dims)


def _split2(x):
    hi = x.astype(BF16)
    return hi, (x - hi.astype(F32)).astype(BF16)


def _dot3(a, b, dims=NN):
    ah, al = _split2(a)
    bh, bl = _split2(b)
    return _mm(ah, bh, dims) + (_mm(al, bh, dims) + _mm(ah, bl, dims))


def _dot_sel(sel, x, dims=NN):
    hi = x.astype(BF16)
    r = x - hi.astype(F32)
    mid = r.astype(BF16)
    lo = (r - mid.astype(F32)).astype(BF16)
    return _mm(sel, hi, dims) + (_mm(sel, mid, dims) + _mm(sel, lo, dims))


def _sigmoid(x):
    return 1.0 / (1.0 + jnp.exp(-x))


def _softplus(x):
    return jnp.maximum(x, 0.0) + jnp.log(1.0 + jnp.exp(-jnp.abs(x)))


def _layer_norm(y, g, b):
    mu = jnp.mean(y, axis=-1, keepdims=True)
    d = y - mu
    var = jnp.mean(d * d, axis=-1, keepdims=True)
    return d * lax.rsqrt(var + LN_EPS) * g + b


def _ada_kernel(c_ref, w_ref, b_ref, o_ref):
    c = c_ref[...]
    o_ref[0] = _dot3(c * _sigmoid(c), w_ref[0]) + b_ref[0]


def _ada_all(c_all, w_ada, b_ada, tn=1024):
    depth, d, n = w_ada.shape
    r = c_all.shape[0]
    return pl.pallas_call(
        _ada_kernel,
        grid=(depth, n // tn),
        in_specs=[pl.BlockSpec((r, d), lambda l, j: (0, 0)),
                  pl.BlockSpec((1, d, tn), lambda l, j: (l, 0, j)),
                  pl.BlockSpec((1, 1, tn), lambda l, j: (l, 0, j))],
        out_specs=pl.BlockSpec((1, r, tn), lambda l, j: (l, 0, j)),
        out_shape=jax.ShapeDtypeStruct((depth, r, n), F32),
        compiler_params=_cp("parallel", "parallel"),
        name="ada_mod",
    )(c_all, w_ada, b_ada.reshape(depth, 1, n))


def _inproj_kernel(x_ref, sh_ref, sc_ref, w_ref, ws_ref, o_ref, os_ref, h_ref):
    @pl.when(pl.program_id(1) == 0)
    def _():
        h = x_ref[...] * (1.0 + sc_ref[...]) + sh_ref[...]
        h_ref[...] = h.astype(BF16)
        os_ref[...] = _dot3(h, ws_ref[...])

    o_ref[...] = _mm(h_ref[...], w_ref[0])


def _inproj(x2d, mod3, w_main, w_small, layer, tm):
    m, d = x2d.shape
    n = w_main.shape[2]
    tn = _tile(n, 512, LANES)
    groups, r, _ = mod3.shape
    tiles_per_group = m // tm // groups
    mod_spec = lambda chunk: pl.BlockSpec((None, r, d), lambda i, j: (i // tiles_per_group, 0, chunk))
    return pl.pallas_call(
        _inproj_kernel,
        grid=(m // tm, n // tn),
        in_specs=[pl.BlockSpec((tm, d), lambda i, j: (i, 0)),
                  mod_spec(0), mod_spec(1),
                  pl.BlockSpec((1, d, tn), lambda i, j: (layer, 0, j)),
                  pl.BlockSpec((None, d, LANES), lambda i, j: (layer, 0, 0))],
        out_specs=[pl.BlockSpec((tm, tn), lambda i, j: (i, j)),
                   pl.BlockSpec((tm, LANES), lambda i, j: (i, 0))],
        out_shape=[jax.ShapeDtypeStruct((m, n), F32), jax.ShapeDtypeStruct((m, LANES), F32)],
        scratch_shapes=[pltpu.VMEM((tm, d), BF16)],
        compiler_params=_cp("parallel", "arbitrary"),
        name="in_proj",
    )(x2d, mod3, mod3, w_main, w_small)


def _gdn_kernel(qkv_ref, z_ref, sm_ref, cprev_ref, s0_ref, cw_ref, alog_ref, dtb_ref, nw_ref,
                o_ref, s_ref, ext_ref, *, n_heads, n_valid, precise):
    C = GDN_CHUNK
    W = n_heads * HEAD_DIM
    dotp = _dot3 if precise else _dot1

    @pl.when(pl.program_id(1) == 0)
    def _():
        ext_ref[0:8, :] = cprev_ref[0]
        s_ref[...] = s0_ref[...]

    ext_ref[8:8 + C, :] = qkv_ref[...]
    cw = cw_ref[...]
    conv = ext_ref[5:5 + C, :] * cw[0:1, :]
    for i in range(1, CONV_W):
        conv = conv + ext_ref[5 + i:5 + i + C, :] * cw[i:i + 1, :]
    u = conv * _sigmoid(conv)
    ext_ref[0:8, :] = ext_ref[C:C + 8, :]

    sm = sm_ref[...]
    beta_all = _sigmoid(sm)
    g_all = -jnp.exp(alog_ref[...]) * _softplus(sm + dtb_ref[...])
    if n_valid < C:
        valid = lax.broadcasted_iota(I32, (C, LANES), 0) < n_valid
        beta_all = jnp.where(valid, beta_all, 0.0)
        g_all = jnp.where(valid, g_all, 0.0)

    P2 = 2 * C
    ii = lax.broadcasted_iota(I32, (P2, P2), 0)
    jj = lax.broadcasted_iota(I32, (P2, P2), 1)
    same = (ii // C) == (jj // C)
    tri_incl = same & (ii >= jj)
    tri_strict = same & (ii > jj)
    tri_bf = jnp.where(tri_incl, 1.0, 0.0).astype(BF16)
    eye = jnp.where(ii == jj, 1.0, 0.0).astype(F32)
    lane0 = jnp.where(jj == 0, 1.0, 0.0).astype(BF16)
    pairs = [(2 * p, 2 * p + 1) for p in range(n_heads // 2)]
    rows = (slice(0, C), slice(C, P2))

    def stack(f, pr):
        return jnp.concatenate([f(pr[0]), f(pr[1])], axis=0)

    def l2n(x):
        return x * lax.rsqrt(jnp.sum(x * x, axis=-1, keepdims=True) + 1e-6)

    head = lambda base: (lambda h: u[:, base + h * HEAD_DIM:base + (h + 1) * HEAD_DIM])
    qn = [l2n(stack(head(0), pr)) * (HEAD_DIM ** -0.5) for pr in pairs]
    kn = [l2n(stack(head(W), pr)) for pr in pairs]
    vv = [stack(head(2 * W), pr) for pr in pairs]
    beta = [stack(lambda h: beta_all[:, h:h + 1], pr) for pr in pairs]
    gb = [stack(lambda h: jnp.broadcast_to(g_all[:, n_heads + h:n_heads + h + 1], (C, LANES)), pr) for pr in pairs]
    gc = [_dot_sel(tri_bf, x) for x in gb]
    grow = [_dot_sel(lane0, x, NT) for x in gc]
    decay = [jnp.where(tri_incl, jnp.exp(jnp.where(tri_incl, c_ - r_, 0.0)), 0.0) for c_, r_ in zip(gc, grow)]
    eg = [jnp.exp(x) for x in gc]
    kk = [dotp(x, x, NT) for x in kn]
    pw = [jnp.where(tri_strict, -(b_ * k_ * d_), 0.0) for b_, k_, d_ in zip(beta, kk, decay)]
    tinv = [eye + x for x in pw]
    for _ in range(C.bit_length() - 2):
        pw = [_dot3(x, x) for x in pw]
        tinv = [t_ + _dot3(t_, x) for t_, x in zip(tinv, pw)]
    sol = [_dot3(t_, jnp.concatenate([b_ * v_, b_ * e_ * k_], axis=1))
           for t_, b_, v_, e_, k_ in zip(tinv, beta, vv, eg, kn)]
    qk = [dotp(q_, k_, NT) * d_ for q_, k_, d_ in zip(qn, kn, decay)]
    qe = [q_ * e_ for q_, e_ in zip(qn, eg)]

    for p, pr in enumerate(pairs):
        s_old = [s_ref[0, h] for h in pr]
        un = [sol[p][rs, :HEAD_DIM] - dotp(sol[p][rs, HEAD_DIM:], s_) for rs, s_ in zip(rows, s_old)]
        un2 = jnp.concatenate(un, axis=0)
        o = jnp.concatenate([dotp(qe[p][rs, :], s_) for rs, s_ in zip(rows, s_old)], axis=0) + dotp(qk[p], un2)
        for h, rs, s_, un_ in zip(pr, rows, s_old, un):
            gch = gc[p][rs, :]
            glast = gch[C - 1:C, :]
            s_ref[0, h] = jnp.exp(glast) * s_ + dotp(kn[p][rs, :] * jnp.exp(glast - gch), un_, TN)
        on = o * lax.rsqrt(jnp.mean(o * o, axis=-1, keepdims=True) + RMS_EPS) * nw_ref[...]
        for h, rs in zip(pr, rows):
            sl = slice(h * HEAD_DIM, (h + 1) * HEAD_DIM)
            zh = z_ref[:, sl]
            o_ref[:, sl] = (on[rs, :] * (zh * _sigmoid(zh))).astype(o_ref.dtype)


def _gdn(proj, small, conv_prev8, s0, conv_w, alog_row, dtb_row, norm_w, n_heads, n_valid, precise):
    C = GDN_CHUNK
    W = n_heads * HEAD_DIM
    B = s0.shape[0]
    nc = proj.shape[0] // B // C
    kern = functools.partial(_gdn_kernel, n_heads=n_heads, n_valid=n_valid, precise=precise)
    full = lambda shape: pl.BlockSpec(shape, lambda b, c: (0,) * len(shape))
    return pl.pallas_call(
        kern,
        grid=(B, nc),
        in_specs=[pl.BlockSpec((C, 3 * W), lambda b, c: (b * nc + c, 0)),
                  pl.BlockSpec((C, W), lambda b, c: (b * nc + c, 3)),
                  pl.BlockSpec((C, LANES), lambda b, c: (b * nc + c, 0)),
                  pl.BlockSpec((1, 8, 3 * W), lambda b, c: (b, 0, 0)),
                  pl.BlockSpec((1, n_heads, HEAD_DIM, HEAD_DIM), lambda b, c: (b, 0, 0, 0)),
                  full((CONV_W, 3 * W)), full((1, LANES)), full((1, LANES)), full((1, HEAD_DIM))],
        out_specs=[pl.BlockSpec((C, W), lambda b, c: (b * nc + c, 0)),
                   pl.BlockSpec((1, n_heads, HEAD_DIM, HEAD_DIM), lambda b, c: (b, 0, 0, 0))],
        out_shape=[jax.ShapeDtypeStruct((B * nc * C, W), BF16),
                   jax.ShapeDtypeStruct((B, n_heads, HEAD_DIM, HEAD_DIM), F32)],
        scratch_shapes=[pltpu.VMEM((C + 8, 3 * W), F32)],
        compiler_params=_cp("parallel", "arbitrary"),
        name="gdn",
    )(proj, proj, small, conv_prev8, s0, conv_w, alog_row, dtb_row, norm_w)


def _rope(x, cos, sin):
    return x * cos + pltpu.roll(x, HEAD_DIM // 2, 1) * sin


def _rope_kernel(q_ref, k_ref, cos_ref, sin_ref, qo_ref, ko_ref, km_ref, *, n_heads):
    cos = cos_ref[...]
    sin = sin_ref[...]
    for h in range(n_heads):
        sl = slice(h * HEAD_DIM, (h + 1) * HEAD_DIM)
        qo_ref[:, sl] = _rope(q_ref[:, sl], cos, sin) * (HEAD_DIM ** -0.5)
        kr = _rope(k_ref[:, sl], cos, sin)
        ko_ref[:, sl] = kr
        km_ref[0, :, sl] = jnp.sum(kr, axis=0, keepdims=True) * (1.0 / MOBA_BLOCK)


def _rope_prompt(proj, cos, sin, n_heads, B):
    W = n_heads * HEAD_DIM
    m = proj.shape[0]
    nb = m // B // MOBA_BLOCK
    return pl.pallas_call(
        functools.partial(_rope_kernel, n_heads=n_heads),
        grid=(B, nb),
        in_specs=[pl.BlockSpec((MOBA_BLOCK, W), lambda b, i: (b * nb + i, 4)),
                  pl.BlockSpec((MOBA_BLOCK, W), lambda b, i: (b * nb + i, 5)),
                  pl.BlockSpec((MOBA_BLOCK, HEAD_DIM), lambda b, i: (i, 0)),
                  pl.BlockSpec((MOBA_BLOCK, HEAD_DIM), lambda b, i: (i, 0))],
        out_specs=[pl.BlockSpec((MOBA_BLOCK, W), lambda b, i: (b * nb + i, 0)),
                   pl.BlockSpec((MOBA_BLOCK, W), lambda b, i: (b * nb + i, 0)),
                   pl.BlockSpec((1, 1, W), lambda b, i: (b * nb + i, 0, 0))],
        out_shape=[jax.ShapeDtypeStruct((m, W), F32), jax.ShapeDtypeStruct((m, W), F32),
                   jax.ShapeDtypeStruct((B * nb, 1, W), F32)],
        compiler_params=_cp("parallel", "parallel"),
        name="rope_prompt",
    )(proj, proj, cos, sin)


def _moba_kernel(q_ref, k_ref, v_ref, km_ref, o_ref, *, nb):
    i = pl.program_id(2)
    tq = MOBA_BLOCK
    q = q_ref[...]
    km = jnp.concatenate([km_ref[0], jnp.zeros((LANES - nb, HEAD_DIM), F32)], axis=0)
    s = _dot3(q, km, NT)
    lane = lax.broadcasted_iota(I32, (tq, LANES), 1)
    past = lane < i
    s = jnp.where(past, s, -jnp.inf)
    rank = jnp.zeros((tq, LANES), I32)
    for j in range(nb):
        sj = s[:, j:j + 1]
        rank = rank + jnp.where((sj > s) | ((sj == s) & (j < lane)), 1, 0)
    sel = jnp.where(past & (rank < MOBA_TOPK), 1.0, 0.0).astype(BF16)
    qb = q.astype(BF16)

    def attend(n):
        K = (n + 1) * tq
        col = lax.broadcasted_iota(I32, (tq, K), 1)
        row = lax.broadcasted_iota(I32, (tq, K), 0)
        allowed = (col >= n * tq) & (col - n * tq <= row)
        if n > 0:
            expand = jnp.where(lax.broadcasted_iota(I32, (LANES, K), 1) // tq
                               == lax.broadcasted_iota(I32, (LANES, K), 0), 1.0, 0.0).astype(BF16)
            allowed = allowed | (_mm(sel, expand) > 0.5)
        logits = jnp.where(allowed, _mm(qb, k_ref[0:K, :].astype(BF16), NT), -jnp.inf)
        mx = jnp.max(logits, axis=-1, keepdims=True)
        p = jnp.exp(logits - mx)
        o = _mm(p.astype(BF16), v_ref[0:K, :].astype(BF16)) / jnp.sum(p, axis=-1, keepdims=True)
        o_ref[...] = o.astype(o_ref.dtype)

    for n in range(nb):
        pl.when(i == n)(functools.partial(attend, n))


def _moba_prompt(q_rot, k_rot, proj, kmean, n_heads, B):
    m, W = q_rot.shape
    S = m // B
    nb = S // MOBA_BLOCK
    return pl.pallas_call(
        functools.partial(_moba_kernel, nb=nb),
        grid=(B, n_heads, nb),
        in_specs=[pl.BlockSpec((MOBA_BLOCK, HEAD_DIM), lambda b, h, i: (b * nb + i, h)),
                  pl.BlockSpec((S, HEAD_DIM), lambda b, h, i: (b, h)),
                  pl.BlockSpec((S, HEAD_DIM), lambda b, h, i: (b, 6 * n_heads + h)),
                  pl.BlockSpec((1, nb, HEAD_DIM), lambda b, h, i: (b, 0, h))],
        out_specs=pl.BlockSpec((MOBA_BLOCK, HEAD_DIM), lambda b, h, i: (b * nb + i, h)),
        out_shape=jax.ShapeDtypeStruct((m, W), BF16),
        compiler_params=_cp("parallel", "parallel", "arbitrary"),
        name="moba_prompt",
    )(q_rot, k_rot, proj, kmean.reshape(B, nb, W))


def _rope_s_kernel(q_ref, k_ref, cos_ref, sin_ref, qo_ref, ko_ref, *, n_heads):
    cos = cos_ref[...]
    sin = sin_ref[...]
    for h in range(n_heads):
        sl = slice(h * HEAD_DIM, (h + 1) * HEAD_DIM)
        qo_ref[:, sl] = _rope(q_ref[:, sl], cos, sin) * (HEAD_DIM ** -0.5)
        ko_ref[:, sl] = _rope(k_ref[:, sl], cos, sin)


def _rope_sample(proj_s, cos, sin, n_heads):
    W = n_heads * HEAD_DIM
    r = proj_s.shape[0]
    return pl.pallas_call(
        functools.partial(_rope_s_kernel, n_heads=n_heads),
        grid=(1,),
        in_specs=[pl.BlockSpec((r, W), lambda i: (0, 4)), pl.BlockSpec((r, W), lambda i: (0, 5)),
                  pl.BlockSpec((1, HEAD_DIM), lambda i: (0, 0)), pl.BlockSpec((1, HEAD_DIM), lambda i: (0, 0))],
        out_specs=[pl.BlockSpec((r, W), lambda i: (0, 0)), pl.BlockSpec((r, W), lambda i: (0, 0))],
        out_shape=[jax.ShapeDtypeStruct((r, W), F32), jax.ShapeDtypeStruct((r, W), F32)],
        compiler_params=_cp("arbitrary"),
        name="rope_sample",
    )(proj_s, proj_s, cos, sin)


def _pool_mean_kernel(pt_ref, *refs, pages_per_step, pages_per_block):
    o_ref = refs[pages_per_step]
    for n in range(pages_per_step // pages_per_block):
        acc = jnp.sum(refs[n * pages_per_block][0, 0], axis=0)
        for t in range(1, pages_per_block):
            acc = acc + jnp.sum(refs[n * pages_per_block + t][0, 0], axis=0)
        o_ref[0, n] = acc * (1.0 / MOBA_BLOCK)


def _pool_block_means(pool_k, page_table, layer, pages_per_step=16):
    _, _, page, H, _ = pool_k.shape
    B, n_pages = page_table.shape
    ppb = MOBA_BLOCK // page
    bps = pages_per_step // ppb
    nblk = n_pages // ppb
    page_spec = lambda t: pl.BlockSpec(
        (1, 1, page, H, HEAD_DIM), lambda b, s, pt: (layer, pt[b, s * pages_per_step + t], 0, 0, 0))
    return pl.pallas_call(
        functools.partial(_pool_mean_kernel, pages_per_step=pages_per_step, pages_per_block=ppb),
        grid_spec=pltpu.PrefetchScalarGridSpec(
            num_scalar_prefetch=1,
            grid=(B, n_pages // pages_per_step),
            in_specs=[page_spec(t) for t in range(pages_per_step)],
            out_specs=pl.BlockSpec((1, bps, H, HEAD_DIM), lambda b, s, pt: (b * (nblk // bps) + s, 0, 0, 0)),
        ),
        out_shape=jax.ShapeDtypeStruct((B * nblk // bps, bps, H, HEAD_DIM), F32),
        compiler_params=_cp("parallel", "arbitrary"),
        name="pool_block_means",
    )(page_table, *([pool_k] * pages_per_step)).reshape(B, nblk, H * HEAD_DIM)


def _sample_select_kernel(q_ref, km_ref, sel_ref, *, n_heads, nblk):
    q = q_ref[0]
    rowi = lax.broadcasted_iota(I32, (n_heads, LANES), 0)
    lane = lax.broadcasted_iota(I32, (n_heads, LANES), 1)
    s = jnp.full((n_heads, LANES), -jnp.inf, F32)
    for h in range(n_heads):
        kmh = jnp.concatenate([km_ref[0, :, h * HEAD_DIM:(h + 1) * HEAD_DIM],
                               jnp.zeros((LANES - nblk, HEAD_DIM), F32)], axis=0)
        s = jnp.where((rowi == h) & (lane < nblk), _dot3(q, kmh, NT), s)
    out = jnp.zeros((n_heads, LANES), I32)
    for t in range(MOBA_TOPK):
        mx = jnp.max(s, axis=-1, keepdims=True)
        idx = jnp.min(jnp.where(s == mx, lane, LANES), axis=-1, keepdims=True)
        out = jnp.where(lane == t, idx, out)
        s = jnp.where(lane == idx, -jnp.inf, s)
    sel_ref[0] = out


def _sample_select(q3, kmean_s):
    B, n_heads, _ = q3.shape
    nblk, W = kmean_s.shape[1:]
    return pl.pallas_call(
        functools.partial(_sample_select_kernel, n_heads=n_heads, nblk=nblk),
        grid=(B,),
        in_specs=[pl.BlockSpec((1, n_heads, HEAD_DIM), lambda b: (b, 0, 0)),
                  pl.BlockSpec((1, nblk, W), lambda b: (b, 0, 0))],
        out_specs=pl.BlockSpec((1, n_heads, LANES), lambda b: (b, 0, 0)),
        out_shape=jax.ShapeDtypeStruct((B, n_heads, LANES), I32),
        compiler_params=_cp("parallel"),
        name="sample_select",
    )(q3, kmean_s)


def _sample_attn_kernel(pg_ref, q_ref, kn_ref, vn_ref, *refs, n_pages):
    kp_refs, vp_refs, o_ref = refs[:n_pages], refs[n_pages:2 * n_pages], refs[2 * n_pages]
    h = pl.program_id(1)
    page = kp_refs[0].shape[2]
    q = q_ref[0, pl.ds(h, 1), :]
    q8 = jnp.broadcast_to(q, (8, HEAD_DIM))
    head = lambda ref: ref[0, 0, :, pl.ds(h, 1), :].reshape(page, HEAD_DIM)
    s_new = jnp.sum(q * kn_ref[0, pl.ds(h, 1), :], axis=-1, keepdims=True)
    s = [_dot3(q8, head(r), NT)[0:1, :] for r in kp_refs]
    mx = s_new
    for x in s:
        mx = jnp.maximum(mx, jnp.max(x, axis=-1, keepdims=True))
    p_new = jnp.exp(s_new - mx)
    den = p_new
    acc = p_new * vn_ref[0, pl.ds(h, 1), :]
    for x, r in zip(s, vp_refs):
        p = jnp.exp(x - mx)
        den = den + jnp.sum(p, axis=-1, keepdims=True)
        acc = acc + _dot3(jnp.broadcast_to(p, (8, page)), head(r))[0:1, :]
    o_ref[0, pl.ds(h, 1), :] = acc / den


def _sample_attn(pages_flat, q3, pool_k, pool_v, kn3, vn3, layer, n_pages):
    B, n_heads, _ = q3.shape
    page = pool_k.shape[2]
    pool_spec = lambda t: pl.BlockSpec(
        (1, 1, page, n_heads, HEAD_DIM), lambda b, h, pg: (layer, pg[(b * n_heads + h) * n_pages + t], 0, 0, 0))
    tok_spec = pl.BlockSpec((1, n_heads, HEAD_DIM), lambda b, h, pg: (b, 0, 0))
    pool_specs = [pool_spec(t) for t in range(n_pages)]
    return pl.pallas_call(
        functools.partial(_sample_attn_kernel, n_pages=n_pages),
        grid_spec=pltpu.PrefetchScalarGridSpec(
            num_scalar_prefetch=1,
            grid=(B, n_heads),
            in_specs=[tok_spec, tok_spec, tok_spec] + pool_specs + pool_specs,
            out_specs=tok_spec,
        ),
        out_shape=jax.ShapeDtypeStruct((B, n_heads, HEAD_DIM), F32),
        compiler_params=_cp("parallel", "arbitrary"),
        name="sample_attn",
    )(pages_flat, q3, kn3, vn3, *([pool_k] * n_pages), *([pool_v] * n_pages))


def _oproj_kernel(oa_ref, ob_ref, wa_ref, wb_ref, x_ref, gt_ref, sc_ref, sh_ref, g_ref, b_ref, wr_ref, br_ref,
                  x1_ref, h2_ref, lg_ref, *, alpha):
    mix = _mm(oa_ref[...], wa_ref[0]) + _mm(ob_ref[...], wb_ref[0])
    x1 = _layer_norm(alpha * x_ref[...] + (1.0 + gt_ref[...]) * mix, g_ref[0], b_ref[0])
    x1_ref[...] = x1
    h2 = x1 * (1.0 + sc_ref[...]) + sh_ref[...]
    h2_ref[...] = h2
    lg_ref[...] = _dot3(h2, wr_ref[0]) + br_ref[0]


def _oproj(o_a, o_b, w_o_bf, x2d, mod3, ln_g, ln_b, w_router, b_router, layer, tm, alpha):
    m, d = x2d.shape
    W = o_a.shape[1]
    groups, r, _ = mod3.shape
    tiles_per_group = m // tm // groups
    mod_spec = lambda chunk: pl.BlockSpec((None, r, d), lambda i: (i // tiles_per_group, 0, chunk))
    lyr = lambda shape: pl.BlockSpec(shape, lambda i: (layer,) + (0,) * (len(shape) - 1))
    return pl.pallas_call(
        functools.partial(_oproj_kernel, alpha=alpha),
        grid=(m // tm,),
        in_specs=[pl.BlockSpec((tm, W), lambda i: (i, 0)), pl.BlockSpec((tm, W), lambda i: (i, 0)),
                  pl.BlockSpec((1, W, d), lambda i: (layer, 0, 0)), pl.BlockSpec((1, W, d), lambda i: (layer, 1, 0)),
                  pl.BlockSpec((tm, d), lambda i: (i, 0)),
                  mod_spec(2), mod_spec(4), mod_spec(3),
                  lyr((1, 1, d)), lyr((1, 1, d)), lyr((1, d, LANES)), lyr((1, 1, LANES))],
        out_specs=[pl.BlockSpec((tm, d), lambda i: (i, 0)), pl.BlockSpec((tm, d), lambda i: (i, 0)),
                   pl.BlockSpec((tm, LANES), lambda i: (i, 0))],
        out_shape=[jax.ShapeDtypeStruct((m, d), F32), jax.ShapeDtypeStruct((m, d), F32),
                   jax.ShapeDtypeStruct((m, LANES), F32)],
        compiler_params=_cp("parallel"),
        name="out_proj",
    )(o_a, o_b, w_o_bf, w_o_bf, x2d, mod3, mod3, mod3, ln_g, ln_b, w_router, b_router)


def _route_kernel(lg_ref, e_ref, g_ref, *, n_exp):
    shape = lg_ref.shape
    lane = lax.broadcasted_iota(I32, shape, 1)
    l = jnp.where(lane < n_exp, lg_ref[...], -jnp.inf)
    vals, idxs = [], []
    for _ in range(TOPK_E):
        mx = jnp.max(l, axis=-1, keepdims=True)
        idx = jnp.min(jnp.where(l == mx, lane, LANES), axis=-1, keepdims=True)
        vals.append(mx)
        idxs.append(idx)
        l = jnp.where(lane == idx, -jnp.inf, l)
    ex = [jnp.exp(v - vals[0]) for v in vals]
    tot = ex[0]
    for e in ex[1:]:
        tot = tot + e
    e_out = jnp.zeros(shape, I32)
    g_out = jnp.zeros(shape, F32)
    for k in range(TOPK_E):
        e_out = jnp.where(lane == k, idxs[k], e_out)
        g_out = jnp.where(lane == k, ex[k] / tot, g_out)
    e_ref[...] = e_out
    g_ref[...] = g_out


def _route(logits, n_exp, tm):
    m = logits.shape[0]
    spec = pl.BlockSpec((tm, LANES), lambda i: (i, 0))
    return pl.pallas_call(
        functools.partial(_route_kernel, n_exp=n_exp),
        grid=(m // tm,),
        in_specs=[spec], out_specs=[spec, spec],
        out_shape=[jax.ShapeDtypeStruct((m, LANES), I32), jax.ShapeDtypeStruct((m, LANES), F32)],
        compiler_params=_cp("parallel"),
        name="route_top4",
    )(logits)


def _rows_copy(src_hbm, idx, dst_ref, row, n, sem):
    return pltpu.make_async_copy(src_hbm.at[pl.ds(idx, n), :], dst_ref.at[pl.ds(row, n), :], sem)


def _moe_kernel(ge_ref, gr_ref, tok_ref, tokn_ref, x_hbm, wg_ref, wu_ref, bg_ref, bu_ref, wd_ref, bd_ref, y_ref,
                xf_ref, xb_ref, sem, *, n_groups, nf):
    g = pl.program_id(0)
    j = pl.program_id(1)
    prefetch_step = min(1, nf - 1)
    rows = gr_ref[g]
    rows_next = jnp.where(g + 1 < n_groups, gr_ref[jnp.minimum(g + 1, n_groups - 1)], 0)
    half = MOE_SUB // 2
    half_tiles = lambda n: (n + half - 1) // half
    n_half = half_tiles(rows)

    def start_gather(slots_ref, n_rows):
        def start(r, c):
            _rows_copy(x_hbm, slots_ref[0, 0, r], xf_ref, r, 1, sem.at[0]).start()
            return c

        lax.fori_loop(0, n_rows, start, 0)

    @pl.when((g == 0) & (j == 0) & (rows > 0))
    def _():
        start_gather(tok_ref, n_half * half)

    @pl.when((rows > 0) & (j == 0))
    def _():
        def finish(s, c):
            _rows_copy(x_hbm, 0, xf_ref, pl.multiple_of(s * half, half), half, sem.at[0]).wait()
            return c

        lax.fori_loop(0, n_half, finish, 0)

        def cast(s, c):
            rs = pl.ds(pl.multiple_of(s * half, half), half)
            xb_ref[rs, :] = xf_ref[rs, :].astype(BF16)
            return c

        lax.fori_loop(0, n_half, cast, 0)

    @pl.when((j == prefetch_step) & (rows_next > 0))
    def _():
        start_gather(tokn_ref, half_tiles(rows_next) * half)

    @pl.when((rows == 0) & (j == 0))
    def _():
        y_ref[...] = jnp.zeros(y_ref.shape, F32)

    @pl.when(rows > 0)
    def _():
        wg = wg_ref[0, 0].astype(BF16)
        wu = wu_ref[0, 0].astype(BF16)
        wd = wd_ref[0, 0].astype(BF16)

        def run(r0, nr):
            rs = slice(r0, r0 + nr)
            xb = xb_ref[rs, :]
            gate = jnp.minimum(_mm(xb, wg) + bg_ref[0, 0], SWIGLU_LIMIT)
            up = jnp.clip(_mm(xb, wu) + bu_ref[0, 0], -SWIGLU_LIMIT, SWIGLU_LIMIT)
            act = (up + 1.0) * gate * _sigmoid(SWIGLU_ALPHA * gate)
            part = _mm(act.astype(BF16), wd)

            @pl.when(j == 0)
            def _():
                y_ref[rs, :] = part + bd_ref[0, 0]

            @pl.when(j > 0)
            def _():
                y_ref[rs, :] += part

        def clear(r0, nr):
            y_ref[r0:r0 + nr, :] = jnp.zeros((nr, y_ref.shape[1]), F32)

        for r in range(MOE_GROUP // MOE_SUB):
            r0 = r * MOE_SUB
            left = rows - r0
            pl.when(left > half)(functools.partial(run, r0, MOE_SUB))
            pl.when((left > 0) & (left <= half))(functools.partial(run, r0, half))
            pl.when((j == 0) & (left > 0) & (left <= half))(functools.partial(clear, r0 + half, half))
            pl.when((j == 0) & (left <= 0))(functools.partial(clear, r0, MOE_SUB))


def _moe_experts(group_e, group_rows, slot_tok, x_rows, w_gu, b_gu4, w_dn, b_dn4, layer):
    n_groups = group_e.shape[0]
    d = x_rows.shape[1]
    f = w_dn.shape[2]
    nf = f // MOE_TF
    jx = lambda j, gr, g: jnp.where(gr[g] > 0, j, nf - 1)
    return pl.pallas_call(
        functools.partial(_moe_kernel, n_groups=n_groups, nf=nf),
        grid_spec=pltpu.PrefetchScalarGridSpec(
            num_scalar_prefetch=2,
            grid=(n_groups, nf),
            in_specs=[
                pl.BlockSpec((1, 1, MOE_GROUP), lambda g, j, ge, gr: (g, 0, 0), memory_space=pltpu.SMEM),
                pl.BlockSpec((1, 1, MOE_GROUP), lambda g, j, ge, gr: (jnp.minimum(g + 1, n_groups - 1), 0, 0),
                             memory_space=pltpu.SMEM),
                pl.BlockSpec(memory_space=pl.ANY),
                pl.BlockSpec((1, 1, d, MOE_TF), lambda g, j, ge, gr: (layer, ge[g], 0, jx(j, gr, g))),
                pl.BlockSpec((1, 1, d, MOE_TF), lambda g, j, ge, gr: (layer, ge[g], 0, nf + jx(j, gr, g))),
                pl.BlockSpec((1, 1, 1, MOE_TF), lambda g, j, ge, gr: (layer, ge[g], 0, jx(j, gr, g))),
                pl.BlockSpec((1, 1, 1, MOE_TF), lambda g, j, ge, gr: (layer, ge[g], 0, nf + jx(j, gr, g))),
                pl.BlockSpec((1, 1, MOE_TF, d), lambda g, j, ge, gr: (layer, ge[g], jx(j, gr, g), 0)),
                pl.BlockSpec((1, 1, 1, d), lambda g, j, ge, gr: (layer, ge[g], 0, 0)),
            ],
            out_specs=pl.BlockSpec((MOE_GROUP, d), lambda g, j, ge, gr: (g, 0)),
            scratch_shapes=[pltpu.VMEM((MOE_GROUP, d), F32), pltpu.VMEM((MOE_GROUP, d), BF16),
                            pltpu.SemaphoreType.DMA((1,))],
        ),
        out_shape=jax.ShapeDtypeStruct((n_groups * MOE_GROUP, d), F32),
        compiler_params=_cp("arbitrary", "arbitrary"),
        name="moe_experts",
    )(group_e, group_rows, slot_tok, slot_tok, x_rows, w_gu, w_gu, b_gu4, b_gu4, w_dn, b_dn4)


def _dispatch(top_e, n_exp):
    m = top_e.shape[0]
    a = m * TOPK_E
    R = MOE_GROUP
    n_groups = -(-a // R) + n_exp
    e_flat = top_e.reshape(a)
    onehot = (e_flat[:, None] == jnp.arange(n_exp, dtype=I32)[None, :]).astype(I32)
    csum = jnp.cumsum(onehot, axis=0)
    rank = jnp.take_along_axis(csum, e_flat[:, None], axis=1)[:, 0] - 1
    counts = csum[-1]
    ng_e = (counts + R - 1) // R
    g_end = jnp.cumsum(ng_e)
    g_start = g_end - ng_e
    dest = g_start[e_flat] * R + rank
    gid = jnp.arange(n_groups, dtype=I32)
    used = gid < g_end[-1]
    last = jnp.maximum(g_end[-1] - 1, 0)
    ge = jnp.minimum(jnp.sum((g_end[None, :] <= jnp.minimum(gid, last)[:, None]).astype(I32), axis=1), n_exp - 1)
    rows = jnp.where(used, jnp.clip(counts[ge] - (gid - g_start[ge]) * R, 0, R), 0).astype(I32)
    tok = jnp.arange(a, dtype=I32) // TOPK_E
    slot_tok = jnp.full((n_groups * R,), m, I32).at[dest].set(tok)
    return dest, slot_tok.reshape(n_groups, 1, R), ge, rows


def _final_kernel(dest_ref, y_hbm, x1_ref, gate_ref, gt_ref, g_ref, b_ref, o_ref, ybuf, sem, *, alpha):
    tm = x1_ref.shape[0]

    def start(r, c):
        for k in range(TOPK_E):
            _rows_copy(y_hbm, dest_ref[0, 0, r * TOPK_E + k], ybuf.at[k], r, 1, sem.at[0]).start()
        return c

    lax.fori_loop(0, tm, start, 0)
    for k in range(TOPK_E):
        _rows_copy(y_hbm, 0, ybuf.at[k], 0, tm, sem.at[0]).wait()
    gates = gate_ref[...]
    moe = ybuf[0] * gates[:, 0:1]
    for k in range(1, TOPK_E):
        moe = moe + ybuf[k] * gates[:, k:k + 1]
    o_ref[...] = _layer_norm(alpha * x1_ref[...] + (1.0 + gt_ref[...]) * moe, g_ref[0], b_ref[0])


def _final(dest2d, y, x1, gates, mod3, ln_g, ln_b, layer, tm, alpha):
    m, d = x1.shape
    groups, r, _ = mod3.shape
    tiles_per_group = m // tm // groups
    row = pl.BlockSpec((tm, d), lambda i: (i, 0))
    lyr = pl.BlockSpec((1, 1, d), lambda i: (layer, 0, 0))
    return pl.pallas_call(
        functools.partial(_final_kernel, alpha=alpha),
        grid=(m // tm,),
        in_specs=[pl.BlockSpec((1, 1, TOPK_E * tm), lambda i: (i, 0, 0), memory_space=pltpu.SMEM),
                  pl.BlockSpec(memory_space=pl.ANY),
                  row, pl.BlockSpec((tm, LANES), lambda i: (i, 0)),
                  pl.BlockSpec((None, r, d), lambda i: (i // tiles_per_group, 0, 5)), lyr, lyr],
        out_specs=row,
        out_shape=jax.ShapeDtypeStruct((m, d), F32),
        scratch_shapes=[pltpu.VMEM((TOPK_E, tm, d), F32), pltpu.SemaphoreType.DMA((1,))],
        compiler_params=_cp("arbitrary"),
        name="final_ln",
    )(dest2d, y, x1, gates, mod3, ln_g, ln_b)


def _rope_tables(pos):
    half = HEAD_DIM // 2
    inv = ROPE_THETA ** (-2.0 * jnp.arange(half, dtype=F32) / HEAD_DIM)
    ang = pos.astype(F32)[:, None] * inv[None, :]
    cos, sin = jnp.cos(ang), jnp.sin(ang)
    return jnp.concatenate([cos, cos], -1), jnp.concatenate([-sin, sin], -1)


def _lane_row(v, offset):
    return jnp.zeros((1, LANES), F32).at[0, offset:offset + v.shape[0]].set(v)


def kernel(x_prompt, x_sample, cache_k, cache_v, state_gdn, state_conv, page_table, c_prompt, c_sample,
           w_in, conv_w, a_log, dt_bias, gdn_norm_w, w_o, w_ada, b_ada, ln1_g, ln1_b, ln2_g, ln2_b,
           w_router, b_router, w_gate_up, b_gate_up, w_down, b_down):
    B, S, D = x_prompt.shape
    Bs, Ss, _ = x_sample.shape
    depth = w_in.shape[0]
    H = a_log.shape[1]
    W = H * HEAD_DIM
    n_exp = w_router.shape[2]
    page = cache_k.shape[2]
    n_pages = page_table.shape[1]
    past = n_pages * page
    RS = SAMPLE_ROWS
    assert Ss == 1 and Bs <= 8 and S % MOBA_BLOCK == 0 and S >= CONV_W - 1
    assert past % MOBA_BLOCK == 0 and past // MOBA_BLOCK >= MOBA_TOPK and MOBA_BLOCK % page == 0
    assert n_exp <= LANES and 2 * H <= LANES and H % 2 == 0 and cache_k.shape[3] == H and w_in.shape[2] == 7 * W + 2 * H
    alpha = (2 * depth) ** 0.25
    MP = B * S

    w_main = jnp.concatenate([w_in[:, :, :4 * W], w_in[:, :, 4 * W + 2 * H:]], axis=-1).astype(BF16)
    w_small = jnp.pad(w_in[:, :, 4 * W:4 * W + 2 * H], ((0, 0), (0, 0), (0, LANES - 2 * H)))
    w_o_bf = w_o.astype(BF16)
    w_router_p = jnp.pad(w_router, ((0, 0), (0, 0), (0, LANES - n_exp)))
    b_router_p = jnp.pad(b_router, ((0, 0), (0, LANES - n_exp))).reshape(depth, 1, LANES)
    ln1_g3, ln1_b3 = ln1_g.reshape(depth, 1, D), ln1_b.reshape(depth, 1, D)
    ln2_g3, ln2_b3 = ln2_g.reshape(depth, 1, D), ln2_b.reshape(depth, 1, D)
    b_gu4 = b_gate_up.reshape(depth, n_exp, 1, -1)
    b_dn4 = b_down.reshape(depth, n_exp, 1, D)
    cos_p, sin_p = _rope_tables(jnp.arange(S, dtype=I32))
    cos_s, sin_s = _rope_tables(jnp.full((1,), past, I32))

    c_all = jnp.concatenate([c_prompt, jnp.zeros((8 - B % 8 if B % 8 else 0, D), F32),
                             c_sample, jnp.zeros((RS - Bs, D), F32)], axis=0)
    s_row0 = c_all.shape[0] - RS
    mod_all = _ada_all(c_all, w_ada, b_ada)

    xp = x_prompt.reshape(MP, D)
    xs = jnp.pad(x_sample.reshape(Bs, D), ((0, RS - Bs), (0, 0)))
    zero_conv = jnp.zeros((B, 8, 3 * W), F32)
    zero_state = jnp.zeros((B, H, HEAD_DIM, HEAD_DIM), F32)
    outs = [[] for _ in range(8)]
    ppb = MOBA_BLOCK // page

    for l in range(depth):
        mod_p = mod_all[l, :B].reshape(B, 1, 6 * D)
        mod_s = mod_all[l, s_row0:].reshape(1, RS, 6 * D)
        alog_row = _lane_row(a_log[l], H)
        dtb_row = _lane_row(dt_bias[l], H)
        nw = gdn_norm_w[l].reshape(1, HEAD_DIM)

        proj_p, small_p = _inproj(xp, mod_p, w_main, w_small, l, tm=_tile(S, 1024, 8))
        oa_p, gdn_p = _gdn(proj_p, small_p, zero_conv, zero_state, conv_w[l], alog_row, dtb_row, nw,
                           H, GDN_CHUNK, False)
        q_rot, k_rot, kmean = _rope_prompt(proj_p, cos_p, sin_p, H, B)
        ob_p = _moba_prompt(q_rot, k_rot, proj_p, kmean, H, B)

        proj_s, small_s = _inproj(xs, mod_s, w_main, w_small, l, tm=RS)
        pad_rows = lambda t: jnp.pad(t[:Bs, None, :], ((0, 0), (0, GDN_CHUNK - 1), (0, 0))).reshape(Bs * GDN_CHUNK, -1)
        conv_prev8 = jnp.pad(state_conv[l], ((0, 0), (8 - (CONV_W - 1), 0), (0, 0)))
        oa_s64, gdn_s = _gdn(pad_rows(proj_s), pad_rows(small_s), conv_prev8, state_gdn[l], conv_w[l],
                             alog_row, dtb_row, nw, H, 1, True)
        oa_s = jnp.pad(oa_s64[::GDN_CHUNK], ((0, RS - Bs), (0, 0)))
        qs_rot, ks_rot = _rope_sample(proj_s, cos_s, sin_s, H)
        q3 = qs_rot[:Bs].reshape(Bs, H, HEAD_DIM)
        kn3 = ks_rot[:Bs].reshape(Bs, H, HEAD_DIM)
        vn3 = proj_s[:Bs, 6 * W:7 * W].reshape(Bs, H, HEAD_DIM)
        kmean_s = _pool_block_means(cache_k, page_table, l)
        sel = _sample_select(q3, kmean_s)[:, :, :MOBA_TOPK]
        lpage = sel[..., None] * ppb + jnp.arange(ppb, dtype=I32)
        pages = jnp.take_along_axis(page_table, lpage.reshape(Bs, -1), axis=1)
        ob_s3 = _sample_attn(pages.reshape(-1).astype(I32), q3, cache_k, cache_v, kn3, vn3, l, MOBA_TOPK * ppb)
        ob_s = jnp.pad(ob_s3.reshape(Bs, W), ((0, RS - Bs), (0, 0))).astype(BF16)

        x1_p, h2_p, lg_p = _oproj(oa_p, ob_p, w_o_bf, xp, mod_p, ln1_g3, ln1_b3, w_router_p, b_router_p, l, 256, alpha)
        x1_s, h2_s, lg_s = _oproj(oa_s, ob_s, w_o_bf, xs, mod_s, ln1_g3, ln1_b3, w_router_p, b_router_p, l, RS, alpha)

        lg_all = jnp.concatenate([lg_p, lg_s], axis=0)
        m_all = MP + RS
        top_e, gates = _route(lg_all, n_exp, _tile(m_all, 1024, 8))
        dest, slot_tok, ge, rows = _dispatch(top_e[:, :TOPK_E], n_exp)
        h2_all = jnp.concatenate([h2_p, h2_s, jnp.zeros((8, D), F32)], axis=0)
        y = _moe_experts(ge, rows, slot_tok, h2_all, w_gate_up, b_gu4, w_down, b_dn4, l)

        tf = _tile(S, 256, 8)
        xp = _final(dest[:MP * TOPK_E].reshape(MP // tf, 1, TOPK_E * tf), y, x1_p, gates[:MP], mod_p,
                    ln2_g3, ln2_b3, l, tf, alpha)
        xs = _final(dest[MP * TOPK_E:].reshape(1, 1, TOPK_E * RS), y, x1_s, gates[MP:], mod_s,
                    ln2_g3, ln2_b3, l, RS, alpha)

        outs[0].append(k_rot.reshape(B, S, H, HEAD_DIM))
        outs[1].append(proj_p[:, 6 * W:7 * W].reshape(B, S, H, HEAD_DIM))
        outs[2].append(kn3.reshape(Bs, 1, H, HEAD_DIM))
        outs[3].append(vn3.reshape(Bs, 1, H, HEAD_DIM))
        outs[4].append(gdn_p)
        outs[5].append(gdn_s)
        outs[6].append(proj_p.reshape(B, S, -1)[:, S - (CONV_W - 1):, :3 * W])
        outs[7].append(jnp.concatenate([state_conv[l][:, 1:], proj_s[:Bs, None, :3 * W]], axis=1))

    return (xp.reshape(B, S, D), xs[:Bs].reshape(Bs, 1, D)) + tuple(jnp.stack(o) for o in outs)
```

```python
import functools

import jax
import jax.numpy as jnp
from jax import lax
from jax.experimental import pallas as pl
from jax.experimental.pallas import tpu as pltpu

F32 = jnp.float32
BF16 = jnp.bfloat16
I32 = jnp.int32

LANES = 128
HEAD_DIM = 128
CONV_W = 4
GDN_CHUNK = 64
MOBA_BLOCK = 256
MOBA_TOPK = 3
ROPE_THETA = 10000.0
TOPK_E = 4
SWIGLU_ALPHA = 1.702
SWIGLU_LIMIT = 7.0
LN_EPS = 1e-5
RMS_EPS = 1e-6
VMEM_LIMIT = 56 * 1024 * 1024
SAMPLE_ROWS = 16
MOE_GROUP = 1280
MOE_SUB = 256
MOE_TF = 256

NN = (((1,), (0,)), ((), ()))
NT = (((1,), (1,)), ((), ()))
TN = (((0,), (0,)), ((), ()))


def _cp(*sem):
    return pltpu.CompilerParams(dimension_semantics=sem, vmem_limit_bytes=VMEM_LIMIT)


def _tile(n, cap, unit):
    return max(t for t in range(unit, min(n, cap) + 1, unit) if n % t == 0)


def _mm(a, b, dims=NN):
    return lax.dot_general(a, b, dims, preferred_element_type=F32)


def _dot1(a, b, dims=NN):
    return _mm(a.astype(BF16), b.astype(BF16), dims)


def _split2(x):
    hi = x.astype(BF16)
    return hi, (x - hi.astype(F32)).astype(BF16)


def _dot3(a, b, dims=NN):
    ah, al = _split2(a)
    bh, bl = _split2(b)
    return _mm(ah, bh, dims) + (_mm(al, bh, dims) + _mm(ah, bl, dims))


def _dot_sel(sel, x, dims=NN):
    hi = x.astype(BF16)
    r = x - hi.astype(F32)
    mid = r.astype(BF16)
    lo = (r - mid.astype(F32)).astype(BF16)
    return _mm(sel, hi, dims) + (_mm(sel, mid, dims) + _mm(sel, lo, dims))


def _sigmoid(x):
    return 1.0 / (1.0 + jnp.exp(-x))


def _softplus(x):
    return jnp.maximum(x, 0.0) + jnp.log(1.0 + jnp.exp(-jnp.abs(x)))


def _layer_norm(y, g, b):
    mu = jnp.mean(y, axis=-1, keepdims=True)
    d = y - mu
    var = jnp.mean(d * d, axis=-1, keepdims=True)
    return d * lax.rsqrt(var + LN_EPS) * g + b


def _ada_kernel(c_ref, w_ref, b_ref, o_ref):
    c = c_ref[...]
    o_ref[0] = _dot3(c * _sigmoid(c), w_ref[0]) + b_ref[0]


def _ada_all(c_all, w_ada, b_ada, tn=1024):
    depth, d, n = w_ada.shape
    r = c_all.shape[0]
    return pl.pallas_call(
        _ada_kernel,
        grid=(depth, n // tn),
        in_specs=[pl.BlockSpec((r, d), lambda l, j: (0, 0)),
                  pl.BlockSpec((1, d, tn), lambda l, j: (l, 0, j)),
                  pl.BlockSpec((1, 1, tn), lambda l, j: (l, 0, j))],
        out_specs=pl.BlockSpec((1, r, tn), lambda l, j: (l, 0, j)),
        out_shape=jax.ShapeDtypeStruct((depth, r, n), F32),
        compiler_params=_cp("parallel", "parallel"),
        name="ada_mod",
    )(c_all, w_ada, b_ada.reshape(depth, 1, n))


def _inproj_kernel(x_ref, sh_ref, sc_ref, w_ref, ws_ref, o_ref, os_ref, h_ref):
    @pl.when(pl.program_id(1) == 0)
    def _():
        h = x_ref[...] * (1.0 + sc_ref[...]) + sh_ref[...]
        h_ref[...] = h.astype(BF16)
        os_ref[...] = _dot3(h, ws_ref[...])

    o_ref[...] = _mm(h_ref[...], w_ref[0])


def _inproj(x2d, mod3, w_main, w_small, layer, tm):
    m, d = x2d.shape
    n = w_main.shape[2]
    tn = _tile(n, 1024, LANES)
    groups, r, _ = mod3.shape
    tiles_per_group = m // tm // groups
    mod_spec = lambda chunk: pl.BlockSpec((None, r, d), lambda i, j: (i // tiles_per_group, 0, chunk))
    return pl.pallas_call(
        _inproj_kernel,
        grid=(m // tm, n // tn),
        in_specs=[pl.BlockSpec((tm, d), lambda i, j: (i, 0)),
                  mod_spec(0), mod_spec(1),
                  pl.BlockSpec((1, d, tn), lambda i, j: (layer, 0, j)),
                  pl.BlockSpec((None, d, LANES), lambda i, j: (layer, 0, 0))],
        out_specs=[pl.BlockSpec((tm, tn), lambda i, j: (i, j)),
                   pl.BlockSpec((tm, LANES), lambda i, j: (i, 0))],
        out_shape=[jax.ShapeDtypeStruct((m, n), F32), jax.ShapeDtypeStruct((m, LANES), F32)],
        scratch_shapes=[pltpu.VMEM((tm, d), BF16)],
        compiler_params=_cp("parallel", "arbitrary"),
        name="in_proj",
    )(x2d, mod3, mod3, w_main, w_small)


def _gdn_kernel(qkv_ref, z_ref, sm_ref, cprev_ref, s0_ref, cw_ref, alog_ref, dtb_ref, nw_ref,
                o_ref, s_ref, ext_ref, *, n_heads, n_valid, precise):
    C = GDN_CHUNK
    W = n_heads * HEAD_DIM
    dotp = _dot3 if precise else _dot1

    @pl.when(pl.program_id(1) == 0)
    def _():
        ext_ref[0:8, :] = cprev_ref[0]
        s_ref[...] = s0_ref[...]

    ext_ref[8:8 + C, :] = qkv_ref[...]
    cw = cw_ref[...]
    conv = ext_ref[5:5 + C, :] * cw[0:1, :]
    for i in range(1, CONV_W):
        conv = conv + ext_ref[5 + i:5 + i + C, :] * cw[i:i + 1, :]
    u = conv * _sigmoid(conv)
    ext_ref[0:8, :] = ext_ref[C:C + 8, :]

    sm = sm_ref[...]
    beta_all = _sigmoid(sm)
    g_all = -jnp.exp(alog_ref[...]) * _softplus(sm + dtb_ref[...])
    if n_valid < C:
        valid = lax.broadcasted_iota(I32, (C, LANES), 0) < n_valid
        beta_all = jnp.where(valid, beta_all, 0.0)
        g_all = jnp.where(valid, g_all, 0.0)

    P2 = 2 * C
    ii = lax.broadcasted_iota(I32, (P2, P2), 0)
    jj = lax.broadcasted_iota(I32, (P2, P2), 1)
    same = (ii // C) == (jj // C)
    tri_incl = same & (ii >= jj)
    tri_strict = same & (ii > jj)
    tri_bf = jnp.where(tri_incl, 1.0, 0.0).astype(BF16)
    eye = jnp.where(ii == jj, 1.0, 0.0).astype(F32)
    lane0 = jnp.where(jj == 0, 1.0, 0.0).astype(BF16)
    pairs = [(2 * p, 2 * p + 1) for p in range(n_heads // 2)]
    rows = (slice(0, C), slice(C, P2))

    def stack(f, pr):
        return jnp.concatenate([f(pr[0]), f(pr[1])], axis=0)

    def l2n(x):
        return x * lax.rsqrt(jnp.sum(x * x, axis=-1, keepdims=True) + 1e-6)

    head = lambda base: (lambda h: u[:, base + h * HEAD_DIM:base + (h + 1) * HEAD_DIM])
    qn = [l2n(stack(head(0), pr)) * (HEAD_DIM ** -0.5) for pr in pairs]
    kn = [l2n(stack(head(W), pr)) for pr in pairs]
    vv = [stack(head(2 * W), pr) for pr in pairs]
    beta = [stack(lambda h: beta_all[:, h:h + 1], pr) for pr in pairs]
    gb = [stack(lambda h: jnp.broadcast_to(g_all[:, n_heads + h:n_heads + h + 1], (C, LANES)), pr) for pr in pairs]
    gc = [_dot_sel(tri_bf, x) for x in gb]
    grow = [_dot_sel(lane0, x, NT) for x in gc]
    decay = [jnp.where(tri_incl, jnp.exp(jnp.where(tri_incl, c_ - r_, 0.0)), 0.0) for c_, r_ in zip(gc, grow)]
    eg = [jnp.exp(x) for x in gc]
    kk = [dotp(x, x, NT) for x in kn]
    pw = [jnp.where(tri_strict, -(b_ * k_ * d_), 0.0) for b_, k_, d_ in zip(beta, kk, decay)]
    tinv = [eye + x for x in pw]
    for _ in range(C.bit_length() - 2):
        pw = [_dot3(x, x) for x in pw]
        tinv = [t_ + _dot3(t_, x) for t_, x in zip(tinv, pw)]
    sol = [_dot3(t_, jnp.concatenate([b_ * v_, b_ * e_ * k_], axis=1))
           for t_, b_, v_, e_, k_ in zip(tinv, beta, vv, eg, kn)]
    qk = [dotp(q_, k_, NT) * d_ for q_, k_, d_ in zip(qn, kn, decay)]
    qe = [q_ * e_ for q_, e_ in zip(qn, eg)]

    for p, pr in enumerate(pairs):
        s_old = [s_ref[0, h] for h in pr]
        un = [sol[p][rs, :HEAD_DIM] - dotp(sol[p][rs, HEAD_DIM:], s_) for rs, s_ in zip(rows, s_old)]
        un2 = jnp.concatenate(un, axis=0)
        o = jnp.concatenate([dotp(qe[p][rs, :], s_) for rs, s_ in zip(rows, s_old)], axis=0) + dotp(qk[p], un2)
        for h, rs, s_, un_ in zip(pr, rows, s_old, un):
            gch = gc[p][rs, :]
            glast = gch[C - 1:C, :]
            s_ref[0, h] = jnp.exp(glast) * s_ + dotp(kn[p][rs, :] * jnp.exp(glast - gch), un_, TN)
        on = o * lax.rsqrt(jnp.mean(o * o, axis=-1, keepdims=True) + RMS_EPS) * nw_ref[...]
        for h, rs in zip(pr, rows):
            sl = slice(h * HEAD_DIM, (h + 1) * HEAD_DIM)
            zh = z_ref[:, sl]
            o_ref[:, sl] = (on[rs, :] * (zh * _sigmoid(zh))).astype(o_ref.dtype)


def _gdn(proj, small, conv_prev8, s0, conv_w, alog_row, dtb_row, norm_w, n_heads, n_valid, precise):
    C = GDN_CHUNK
    W = n_heads * HEAD_DIM
    B = s0.shape[0]
    nc = proj.shape[0] // B // C
    kern = functools.partial(_gdn_kernel, n_heads=n_heads, n_valid=n_valid, precise=precise)
    full = lambda shape: pl.BlockSpec(shape, lambda b, c: (0,) * len(shape))
    return pl.pallas_call(
        kern,
        grid=(B, nc),
        in_specs=[pl.BlockSpec((C, 3 * W), lambda b, c: (b * nc + c, 0)),
                  pl.BlockSpec((C, W), lambda b, c: (b * nc + c, 3)),
                  pl.BlockSpec((C, LANES), lambda b, c: (b * nc + c, 0)),
                  pl.BlockSpec((1, 8, 3 * W), lambda b, c: (b, 0, 0)),
                  pl.BlockSpec((1, n_heads, HEAD_DIM, HEAD_DIM), lambda b, c: (b, 0, 0, 0)),
                  full((CONV_W, 3 * W)), full((1, LANES)), full((1, LANES)), full((1, HEAD_DIM))],
        out_specs=[pl.BlockSpec((C, W), lambda b, c: (b * nc + c, 0)),
                   pl.BlockSpec((1, n_heads, HEAD_DIM, HEAD_DIM), lambda b, c: (b, 0, 0, 0))],
        out_shape=[jax.ShapeDtypeStruct((B * nc * C, W), BF16),
                   jax.ShapeDtypeStruct((B, n_heads, HEAD_DIM, HEAD_DIM), F32)],
        scratch_shapes=[pltpu.VMEM((C + 8, 3 * W), F32)],
        compiler_params=_cp("parallel", "arbitrary"),
        name="gdn",
    )(proj, proj, small, conv_prev8, s0, conv_w, alog_row, dtb_row, norm_w)


def _rope(x, cos, sin):
    return x * cos + pltpu.roll(x, HEAD_DIM // 2, 1) * sin


def _rope_kernel(q_ref, k_ref, cos_ref, sin_ref, qo_ref, ko_ref, km_ref, *, n_heads):
    cos = cos_ref[...]
    sin = sin_ref[...]
    for h in range(n_heads):
        sl = slice(h * HEAD_DIM, (h + 1) * HEAD_DIM)
        qo_ref[:, sl] = _rope(q_ref[:, sl], cos, sin) * (HEAD_DIM ** -0.5)
        kr = _rope(k_ref[:, sl], cos, sin)
        ko_ref[:, sl] = kr
        km_ref[0, :, sl] = jnp.sum(kr, axis=0, keepdims=True) * (1.0 / MOBA_BLOCK)


def _rope_prompt(proj, cos, sin, n_heads, B):
    W = n_heads * HEAD_DIM
    m = proj.shape[0]
    nb = m // B // MOBA_BLOCK
    return pl.pallas_call(
        functools.partial(_rope_kernel, n_heads=n_heads),
        grid=(B, nb),
        in_specs=[pl.BlockSpec((MOBA_BLOCK, W), lambda b, i: (b * nb + i, 4)),
                  pl.BlockSpec((MOBA_BLOCK, W), lambda b, i: (b * nb + i, 5)),
                  pl.BlockSpec((MOBA_BLOCK, HEAD_DIM), lambda b, i: (i, 0)),
                  pl.BlockSpec((MOBA_BLOCK, HEAD_DIM), lambda b, i: (i, 0))],
        out_specs=[pl.BlockSpec((MOBA_BLOCK, W), lambda b, i: (b * nb + i, 0)),
                   pl.BlockSpec((MOBA_BLOCK, W), lambda b, i: (b * nb + i, 0)),
                   pl.BlockSpec((1, 1, W), lambda b, i: (b * nb + i, 0, 0))],
        out_shape=[jax.ShapeDtypeStruct((m, W), F32), jax.ShapeDtypeStruct((m, W), F32),
                   jax.ShapeDtypeStruct((B * nb, 1, W), F32)],
        compiler_params=_cp("parallel", "parallel"),
        name="rope_prompt",
    )(proj, proj, cos, sin)


def _moba_kernel(q_ref, k_ref, v_ref, km_ref, o_ref, *, nb):
    i = pl.program_id(2)
    tq = MOBA_BLOCK
    q = q_ref[...]
    km = jnp.concatenate([km_ref[0], jnp.zeros((LANES - nb, HEAD_DIM), F32)], axis=0)
    s = _dot3(q, km, NT)
    lane = lax.broadcasted_iota(I32, (tq, LANES), 1)
    past = lane < i
    s = jnp.where(past, s, -jnp.inf)
    rank = jnp.zeros((tq, LANES), I32)
    for j in range(nb):
        sj = s[:, j:j + 1]
        rank = rank + jnp.where((sj > s) | ((sj == s) & (j < lane)), 1, 0)
    sel = jnp.where(past & (rank < MOBA_TOPK), 1.0, 0.0).astype(BF16)
    qb = q.astype(BF16)

    def attend(n):
        K = (n + 1) * tq
        col = lax.broadcasted_iota(I32, (tq, K), 1)
        row = lax.broadcasted_iota(I32, (tq, K), 0)
        allowed = (col >= n * tq) & (col - n * tq <= row)
        if n > 0:
            expand = jnp.where(lax.broadcasted_iota(I32, (LANES, K), 1) // tq
                               == lax.broadcasted_iota(I32, (LANES, K), 0), 1.0, 0.0).astype(BF16)
            allowed = allowed | (_mm(sel, expand) > 0.5)
        logits = jnp.where(allowed, _mm(qb, k_ref[0:K, :].astype(BF16), NT), -jnp.inf)
        mx = jnp.max(logits, axis=-1, keepdims=True)
        p = jnp.exp(logits - mx)
        o = _mm(p.astype(BF16), v_ref[0:K, :].astype(BF16)) / jnp.sum(p, axis=-1, keepdims=True)
        o_ref[...] = o.astype(o_ref.dtype)

    for n in range(nb):
        pl.when(i == n)(functools.partial(attend, n))


def _moba_prompt(q_rot, k_rot, proj, kmean, n_heads, B):
    m, W = q_rot.shape
    S = m // B
    nb = S // MOBA_BLOCK
    return pl.pallas_call(
        functools.partial(_moba_kernel, nb=nb),
        grid=(B, n_heads, nb),
        in_specs=[pl.BlockSpec((MOBA_BLOCK, HEAD_DIM), lambda b, h, i: (b * nb + i, h)),
                  pl.BlockSpec((S, HEAD_DIM), lambda b, h, i: (b, h)),
                  pl.BlockSpec((S, HEAD_DIM), lambda b, h, i: (b, 6 * n_heads + h)),
                  pl.BlockSpec((1, nb, HEAD_DIM), lambda b, h, i: (b, 0, h))],
        out_specs=pl.BlockSpec((MOBA_BLOCK, HEAD_DIM), lambda b, h, i: (b * nb + i, h)),
        out_shape=jax.ShapeDtypeStruct((m, W), BF16),
        compiler_params=_cp("parallel", "parallel", "arbitrary"),
        name="moba_prompt",
    )(q_rot, k_rot, proj, kmean.reshape(B, nb, W))


def _rope_s_kernel(q_ref, k_ref, cos_ref, sin_ref, qo_ref, ko_ref, *, n_heads):
    cos = cos_ref[...]
    sin = sin_ref[...]
    for h in range(n_heads):
        sl = slice(h * HEAD_DIM, (h + 1) * HEAD_DIM)
        qo_ref[:, sl] = _rope(q_ref[:, sl], cos, sin) * (HEAD_DIM ** -0.5)
        ko_ref[:, sl] = _rope(k_ref[:, sl], cos, sin)


def _rope_sample(proj_s, cos, sin, n_heads):
    W = n_heads * HEAD_DIM
    r = proj_s.shape[0]
    return pl.pallas_call(
        functools.partial(_rope_s_kernel, n_heads=n_heads),
        grid=(1,),
        in_specs=[pl.BlockSpec((r, W), lambda i: (0, 4)), pl.BlockSpec((r, W), lambda i: (0, 5)),
                  pl.BlockSpec((1, HEAD_DIM), lambda i: (0, 0)), pl.BlockSpec((1, HEAD_DIM), lambda i: (0, 0))],
        out_specs=[pl.BlockSpec((r, W), lambda i: (0, 0)), pl.BlockSpec((r, W), lambda i: (0, 0))],
        out_shape=[jax.ShapeDtypeStruct((r, W), F32), jax.ShapeDtypeStruct((r, W), F32)],
        compiler_params=_cp("arbitrary"),
        name="rope_sample",
    )(proj_s, proj_s, cos, sin)


def _pool_mean_kernel(pt_ref, *refs, pages_per_step, pages_per_block):
    o_ref = refs[pages_per_step]
    for n in range(pages_per_step // pages_per_block):
        acc = jnp.sum(refs[n * pages_per_block][0, 0], axis=0)
        for t in range(1, pages_per_block):
            acc = acc + jnp.sum(refs[n * pages_per_block + t][0, 0], axis=0)
        o_ref[0, n] = acc * (1.0 / MOBA_BLOCK)


def _pool_block_means(pool_k, page_table, layer, pages_per_step=16):
    _, _, page, H, _ = pool_k.shape
    B, n_pages = page_table.shape
    ppb = MOBA_BLOCK // page
    bps = pages_per_step // ppb
    nblk = n_pages // ppb
    page_spec = lambda t: pl.BlockSpec(
        (1, 1, page, H, HEAD_DIM), lambda b, s, pt: (layer, pt[b, s * pages_per_step + t], 0, 0, 0))
    return pl.pallas_call(
        functools.partial(_pool_mean_kernel, pages_per_step=pages_per_step, pages_per_block=ppb),
        grid_spec=pltpu.PrefetchScalarGridSpec(
            num_scalar_prefetch=1,
            grid=(B, n_pages // pages_per_step),
            in_specs=[page_spec(t) for t in range(pages_per_step)],
            out_specs=pl.BlockSpec((1, bps, H, HEAD_DIM), lambda b, s, pt: (b * (nblk // bps) + s, 0, 0, 0)),
        ),
        out_shape=jax.ShapeDtypeStruct((B * nblk // bps, bps, H, HEAD_DIM), F32),
        compiler_params=_cp("parallel", "arbitrary"),
        name="pool_block_means",
    )(page_table, *([pool_k] * pages_per_step)).reshape(B, nblk, H * HEAD_DIM)


def _sample_select_kernel(q_ref, km_ref, sel_ref, *, n_heads, nblk):
    q = q_ref[0]
    rowi = lax.broadcasted_iota(I32, (n_heads, LANES), 0)
    lane = lax.broadcasted_iota(I32, (n_heads, LANES), 1)
    s = jnp.full((n_heads, LANES), -jnp.inf, F32)
    for h in range(n_heads):
        kmh = jnp.concatenate([km_ref[0, :, h * HEAD_DIM:(h + 1) * HEAD_DIM],
                               jnp.zeros((LANES - nblk, HEAD_DIM), F32)], axis=0)
        s = jnp.where((rowi == h) & (lane < nblk), _dot3(q, kmh, NT), s)
    out = jnp.zeros((n_heads, LANES), I32)
    for t in range(MOBA_TOPK):
        mx = jnp.max(s, axis=-1, keepdims=True)
        idx = jnp.min(jnp.where(s == mx, lane, LANES), axis=-1, keepdims=True)
        out = jnp.where(lane == t, idx, out)
        s = jnp.where(lane == idx, -jnp.inf, s)
    sel_ref[0] = out


def _sample_select(q3, kmean_s):
    B, n_heads, _ = q3.shape
    nblk, W = kmean_s.shape[1:]
    return pl.pallas_call(
        functools.partial(_sample_select_kernel, n_heads=n_heads, nblk=nblk),
        grid=(B,),
        in_specs=[pl.BlockSpec((1, n_heads, HEAD_DIM), lambda b: (b, 0, 0)),
                  pl.BlockSpec((1, nblk, W), lambda b: (b, 0, 0))],
        out_specs=pl.BlockSpec((1, n_heads, LANES), lambda b: (b, 0, 0)),
        out_shape=jax.ShapeDtypeStruct((B, n_heads, LANES), I32),
        compiler_params=_cp("parallel"),
        name="sample_select",
    )(q3, kmean_s)


def _sample_attn_kernel(pg_ref, q_ref, kn_ref, vn_ref, *refs, n_pages):
    kp_refs, vp_refs, o_ref = refs[:n_pages], refs[n_pages:2 * n_pages], refs[2 * n_pages]
    h = pl.program_id(1)
    page = kp_refs[0].shape[2]
    q = q_ref[0, pl.ds(h, 1), :]
    q8 = jnp.broadcast_to(q, (8, HEAD_DIM))
    head = lambda ref: ref[0, 0, :, pl.ds(h, 1), :].reshape(page, HEAD_DIM)
    s_new = jnp.sum(q * kn_ref[0, pl.ds(h, 1), :], axis=-1, keepdims=True)
    s = [_dot3(q8, head(r), NT)[0:1, :] for r in kp_refs]
    mx = s_new
    for x in s:
        mx = jnp.maximum(mx, jnp.max(x, axis=-1, keepdims=True))
    p_new = jnp.exp(s_new - mx)
    den = p_new
    acc = p_new * vn_ref[0, pl.ds(h, 1), :]
    for x, r in zip(s, vp_refs):
        p = jnp.exp(x - mx)
        den = den + jnp.sum(p, axis=-1, keepdims=True)
        acc = acc + _dot3(jnp.broadcast_to(p, (8, page)), head(r))[0:1, :]
    o_ref[0, pl.ds(h, 1), :] = acc / den


def _sample_attn(pages_flat, q3, pool_k, pool_v, kn3, vn3, layer, n_pages):
    B, n_heads, _ = q3.shape
    page = pool_k.shape[2]
    pool_spec = lambda t: pl.BlockSpec(
        (1, 1, page, n_heads, HEAD_DIM), lambda b, h, pg: (layer, pg[(b * n_heads + h) * n_pages + t], 0, 0, 0))
    tok_spec = pl.BlockSpec((1, n_heads, HEAD_DIM), lambda b, h, pg: (b, 0, 0))
    pool_specs = [pool_spec(t) for t in range(n_pages)]
    return pl.pallas_call(
        functools.partial(_sample_attn_kernel, n_pages=n_pages),
        grid_spec=pltpu.PrefetchScalarGridSpec(
            num_scalar_prefetch=1,
            grid=(B, n_heads),
            in_specs=[tok_spec, tok_spec, tok_spec] + pool_specs + pool_specs,
            out_specs=tok_spec,
        ),
        out_shape=jax.ShapeDtypeStruct((B, n_heads, HEAD_DIM), F32),
        compiler_params=_cp("parallel", "arbitrary"),
        name="sample_attn",
    )(pages_flat, q3, kn3, vn3, *([pool_k] * n_pages), *([pool_v] * n_pages))


def _oproj_kernel(oa_ref, ob_ref, wa_ref, wb_ref, x_ref, gt_ref, sc_ref, sh_ref, g_ref, b_ref, wr_ref, br_ref,
                  x1_ref, h2_ref, lg_ref, *, alpha):
    mix = _mm(oa_ref[...], wa_ref[0]) + _mm(ob_ref[...], wb_ref[0])
    x1 = _layer_norm(alpha * x_ref[...] + (1.0 + gt_ref[...]) * mix, g_ref[0], b_ref[0])
    x1_ref[...] = x1
    h2 = x1 * (1.0 + sc_ref[...]) + sh_ref[...]
    h2_ref[...] = h2
    lg_ref[...] = _dot3(h2, wr_ref[0]) + br_ref[0]


def _oproj(o_a, o_b, w_o_bf, x2d, mod3, ln_g, ln_b, w_router, b_router, layer, tm, alpha):
    m, d = x2d.shape
    W = o_a.shape[1]
    groups, r, _ = mod3.shape
    tiles_per_group = m // tm // groups
    mod_spec = lambda chunk: pl.BlockSpec((None, r, d), lambda i: (i // tiles_per_group, 0, chunk))
    lyr = lambda shape: pl.BlockSpec(shape, lambda i: (layer,) + (0,) * (len(shape) - 1))
    return pl.pallas_call(
        functools.partial(_oproj_kernel, alpha=alpha),
        grid=(m // tm,),
        in_specs=[pl.BlockSpec((tm, W), lambda i: (i, 0)), pl.BlockSpec((tm, W), lambda i: (i, 0)),
                  pl.BlockSpec((1, W, d), lambda i: (layer, 0, 0)), pl.BlockSpec((1, W, d), lambda i: (layer, 1, 0)),
                  pl.BlockSpec((tm, d), lambda i: (i, 0)),
                  mod_spec(2), mod_spec(4), mod_spec(3),
                  lyr((1, 1, d)), lyr((1, 1, d)), lyr((1, d, LANES)), lyr((1, 1, LANES))],
        out_specs=[pl.BlockSpec((tm, d), lambda i: (i, 0)), pl.BlockSpec((tm, d), lambda i: (i, 0)),
                   pl.BlockSpec((tm, LANES), lambda i: (i, 0))],
        out_shape=[jax.ShapeDtypeStruct((m, d), F32), jax.ShapeDtypeStruct((m, d), F32),
                   jax.ShapeDtypeStruct((m, LANES), F32)],
        compiler_params=_cp("parallel"),
        name="out_proj",
    )(o_a, o_b, w_o_bf, w_o_bf, x2d, mod3, mod3, mod3, ln_g, ln_b, w_router, b_router)


def _route_kernel(lg_ref, e_ref, g_ref, *, n_exp):
    shape = lg_ref.shape
    lane = lax.broadcasted_iota(I32, shape, 1)
    l = jnp.where(lane < n_exp, lg_ref[...], -jnp.inf)
    vals, idxs = [], []
    for _ in range(TOPK_E):
        mx = jnp.max(l, axis=-1, keepdims=True)
        idx = jnp.min(jnp.where(l == mx, lane, LANES), axis=-1, keepdims=True)
        vals.append(mx)
        idxs.append(idx)
        l = jnp.where(lane == idx, -jnp.inf, l)
    ex = [jnp.exp(v - vals[0]) for v in vals]
    tot = ex[0]
    for e in ex[1:]:
        tot = tot + e
    e_out = jnp.zeros(shape, I32)
    g_out = jnp.zeros(shape, F32)
    for k in range(TOPK_E):
        e_out = jnp.where(lane == k, idxs[k], e_out)
        g_out = jnp.where(lane == k, ex[k] / tot, g_out)
    e_ref[...] = e_out
    g_ref[...] = g_out


def _route(logits, n_exp, tm):
    m = logits.shape[0]
    spec = pl.BlockSpec((tm, LANES), lambda i: (i, 0))
    return pl.pallas_call(
        functools.partial(_route_kernel, n_exp=n_exp),
        grid=(m // tm,),
        in_specs=[spec], out_specs=[spec, spec],
        out_shape=[jax.ShapeDtypeStruct((m, LANES), I32), jax.ShapeDtypeStruct((m, LANES), F32)],
        compiler_params=_cp("parallel"),
        name="route_top4",
    )(logits)


def _rows_copy(src_hbm, idx, dst_ref, row, n, sem):
    return pltpu.make_async_copy(src_hbm.at[pl.ds(idx, n), :], dst_ref.at[pl.ds(row, n), :], sem)


def _moe_kernel(ge_ref, gr_ref, tok_ref, tokn_ref, x_hbm, wg_ref, wu_ref, bg_ref, bu_ref, wd_ref, bd_ref, y_ref,
                xf_ref, xb_ref, sem, *, n_groups, nf):
    g = pl.program_id(0)
    j = pl.program_id(1)
    rows = gr_ref[g]
    rows_next = jnp.where(g + 1 < n_groups, gr_ref[jnp.minimum(g + 1, n_groups - 1)], 0)
    live_next = rows_next > 0
    half = MOE_SUB // 2
    chunk = MOE_GROUP // nf

    def start_rows(slots_ref, first, n, unrolled):
        def start(r, c):
            _rows_copy(x_hbm, slots_ref[0, 0, r], xf_ref, r, 1, sem.at[0]).start()
            return c

        if unrolled:
            for k in range(n):
                start(first + k, 0)
        else:
            lax.fori_loop(first, first + n, start, 0)

    @pl.when((g == 0) & (j == 0) & (rows > 0))
    def _():
        start_rows(tok_ref, 0, MOE_GROUP, False)

    @pl.when((rows > 0) & (j == 0))
    def _():
        for s in range(MOE_GROUP // half):
            _rows_copy(x_hbm, 0, xf_ref, s * half, half, sem.at[0]).wait()

        def cast(s, c):
            rs = pl.ds(pl.multiple_of(s * half, half), half)
            xb_ref[rs, :] = xf_ref[rs, :].astype(BF16)
            return c

        lax.fori_loop(0, (rows + half - 1) // half, cast, 0)

    @pl.when((rows == 0) & live_next & (j == 0))
    def _():
        start_rows(tokn_ref, 0, MOE_GROUP, False)

    @pl.when((rows == 0) & (j == 0))
    def _():
        y_ref[...] = jnp.zeros(y_ref.shape, F32)

    @pl.when(rows > 0)
    def _():
        wg = wg_ref[0, 0].astype(BF16)
        wu = wu_ref[0, 0].astype(BF16)
        wd = wd_ref[0, 0].astype(BF16)

        def run(r0, nr, stage_next=False):
            if stage_next:
                start_rows(tokn_ref, j * chunk, chunk, True)
            rs = slice(r0, r0 + nr)
            xb = xb_ref[rs, :]
            gate = jnp.minimum(_mm(xb, wg) + bg_ref[0, 0], SWIGLU_LIMIT)
            up = jnp.clip(_mm(xb, wu) + bu_ref[0, 0], -SWIGLU_LIMIT, SWIGLU_LIMIT)
            act = (up + 1.0) * gate * _sigmoid(SWIGLU_ALPHA * gate)
            part = _mm(act.astype(BF16), wd)

            @pl.when(j == 0)
            def _():
                y_ref[rs, :] = part + bd_ref[0, 0]

            @pl.when(j > 0)
            def _():
                y_ref[rs, :] += part

        def clear(r0, nr):
            y_ref[r0:r0 + nr, :] = jnp.zeros((nr, y_ref.shape[1]), F32)

        pl.when((rows > half) & live_next)(functools.partial(run, 0, MOE_SUB, True))
        pl.when((rows > half) & ~live_next)(functools.partial(run, 0, MOE_SUB))
        pl.when((rows <= half) & live_next)(functools.partial(run, 0, half, True))
        pl.when((rows <= half) & ~live_next)(functools.partial(run, 0, half))
        for r in range(MOE_GROUP // MOE_SUB):
            r0 = r * MOE_SUB
            left = rows - r0
            if r > 0:
                pl.when(left > half)(functools.partial(run, r0, MOE_SUB))
                pl.when((left > 0) & (left <= half))(functools.partial(run, r0, half))
            pl.when((j == 0) & (left > 0) & (left <= half))(functools.partial(clear, r0 + half, half))
            pl.when((j == 0) & (left <= 0))(functools.partial(clear, r0, MOE_SUB))


def _moe_experts(group_e, group_rows, slot_tok, x_rows, w_gu, b_gu4, w_dn, b_dn4, layer):
    n_groups = group_e.shape[0]
    d = x_rows.shape[1]
    f = w_dn.shape[2]
    nf = f // MOE_TF
    assert MOE_GROUP % nf == 0
    jx = lambda j, gr, g: jnp.where(gr[g] > 0, j, nf - 1)
    return pl.pallas_call(
        functools.partial(_moe_kernel, n_groups=n_groups, nf=nf),
        grid_spec=pltpu.PrefetchScalarGridSpec(
            num_scalar_prefetch=2,
            grid=(n_groups, nf),
            in_specs=[
                pl.BlockSpec((1, 1, MOE_GROUP), lambda g, j, ge, gr: (g, 0, 0), memory_space=pltpu.SMEM),
                pl.BlockSpec((1, 1, MOE_GROUP), lambda g, j, ge, gr: (jnp.minimum(g + 1, n_groups - 1), 0, 0),
                             memory_space=pltpu.SMEM),
                pl.BlockSpec(memory_space=pl.ANY),
                pl.BlockSpec((1, 1, d, MOE_TF), lambda g, j, ge, gr: (layer, ge[g], 0, jx(j, gr, g))),
                pl.BlockSpec((1, 1, d, MOE_TF), lambda g, j, ge, gr: (layer, ge[g], 0, nf + jx(j, gr, g))),
                pl.BlockSpec((1, 1, 1, MOE_TF), lambda g, j, ge, gr: (layer, ge[g], 0, jx(j, gr, g))),
                pl.BlockSpec((1, 1, 1, MOE_TF), lambda g, j, ge, gr: (layer, ge[g], 0, nf + jx(j, gr, g))),
                pl.BlockSpec((1, 1, MOE_TF, d), lambda g, j, ge, gr: (layer, ge[g], jx(j, gr, g), 0)),
                pl.BlockSpec((1, 1, 1, d), lambda g, j, ge, gr: (layer, ge[g], 0, 0)),
            ],
            out_specs=pl.BlockSpec((MOE_GROUP, d), lambda g, j, ge, gr: (g, 0)),
            scratch_shapes=[pltpu.VMEM((MOE_GROUP, d), F32), pltpu.VMEM((MOE_GROUP, d), BF16),
                            pltpu.SemaphoreType.DMA((1,))],
        ),
        out_shape=jax.ShapeDtypeStruct((n_groups * MOE_GROUP, d), F32),
        compiler_params=_cp("arbitrary", "arbitrary"),
        name="moe_experts",
    )(group_e, group_rows, slot_tok, slot_tok, x_rows, w_gu, w_gu, b_gu4, b_gu4, w_dn, b_dn4)


def _dispatch(top_e, n_exp):
    m = top_e.shape[0]
    a = m * TOPK_E
    R = MOE_GROUP
    n_groups = -(-a // R) + n_exp
    e_flat = top_e.reshape(a)
    onehot = (e_flat[:, None] == jnp.arange(n_exp, dtype=I32)[None, :]).astype(I32)
    csum = jnp.cumsum(onehot, axis=0)
    rank = jnp.take_along_axis(csum, e_flat[:, None], axis=1)[:, 0] - 1
    counts = csum[-1]
    ng_e = (counts + R - 1) // R
    g_end = jnp.cumsum(ng_e)
    g_start = g_end - ng_e
    dest = g_start[e_flat] * R + rank
    gid = jnp.arange(n_groups, dtype=I32)
    used = gid < g_end[-1]
    last = jnp.maximum(g_end[-1] - 1, 0)
    ge = jnp.minimum(jnp.sum((g_end[None, :] <= jnp.minimum(gid, last)[:, None]).astype(I32), axis=1), n_exp - 1)
    rows = jnp.where(used, jnp.clip(counts[ge] - (gid - g_start[ge]) * R, 0, R), 0).astype(I32)
    tok = jnp.arange(a, dtype=I32) // TOPK_E
    slot_tok = jnp.full((n_groups * R,), m, I32).at[dest].set(tok)
    return dest, slot_tok.reshape(n_groups, 1, R), ge, rows


def _final_kernel(dest_ref, y_hbm, x1_ref, gate_ref, gt_ref, g_ref, b_ref, o_ref, ybuf, sem, *, alpha):
    tm = x1_ref.shape[0]

    def start(r, c):
        for k in range(TOPK_E):
            _rows_copy(y_hbm, dest_ref[0, 0, r * TOPK_E + k], ybuf.at[k], r, 1, sem.at[0]).start()
        return c

    lax.fori_loop(0, tm, start, 0)
    for k in range(TOPK_E):
        _rows_copy(y_hbm, 0, ybuf.at[k], 0, tm, sem.at[0]).wait()
    gates = gate_ref[...]
    moe = ybuf[0] * gates[:, 0:1]
    for k in range(1, TOPK_E):
        moe = moe + ybuf[k] * gates[:, k:k + 1]
    o_ref[...] = _layer_norm(alpha * x1_ref[...] + (1.0 + gt_ref[...]) * moe, g_ref[0], b_ref[0])


def _final(dest2d, y, x1, gates, mod3, ln_g, ln_b, layer, tm, alpha):
    m, d = x1.shape
    groups, r, _ = mod3.shape
    tiles_per_group = m // tm // groups
    row = pl.BlockSpec((tm, d), lambda i: (i, 0))
    lyr = pl.BlockSpec((1, 1, d), lambda i: (layer, 0, 0))
    return pl.pallas_call(
        functools.partial(_final_kernel, alpha=alpha),
        grid=(m // tm,),
        in_specs=[pl.BlockSpec((1, 1, TOPK_E * tm), lambda i: (i, 0, 0), memory_space=pltpu.SMEM),
                  pl.BlockSpec(memory_space=pl.ANY),
                  row, pl.BlockSpec((tm, LANES), lambda i: (i, 0)),
                  pl.BlockSpec((None, r, d), lambda i: (i // tiles_per_group, 0, 5)), lyr, lyr],
        out_specs=row,
        out_shape=jax.ShapeDtypeStruct((m, d), F32),
        scratch_shapes=[pltpu.VMEM((TOPK_E, tm, d), F32), pltpu.SemaphoreType.DMA((1,))],
        compiler_params=_cp("arbitrary"),
        name="final_ln",
    )(dest2d, y, x1, gates, mod3, ln_g, ln_b)


def _rope_tables(pos):
    half = HEAD_DIM // 2
    inv = ROPE_THETA ** (-2.0 * jnp.arange(half, dtype=F32) / HEAD_DIM)
    ang = pos.astype(F32)[:, None] * inv[None, :]
    cos, sin = jnp.cos(ang), jnp.sin(ang)
    return jnp.concatenate([cos, cos], -1), jnp.concatenate([-sin, sin], -1)


def _lane_row(v, offset):
    return jnp.zeros((1, LANES), F32).at[0, offset:offset + v.shape[0]].set(v)


def kernel(x_prompt, x_sample, cache_k, cache_v, state_gdn, state_conv, page_table, c_prompt, c_sample,
           w_in, conv_w, a_log, dt_bias, gdn_norm_w, w_o, w_ada, b_ada, ln1_g, ln1_b, ln2_g, ln2_b,
           w_router, b_router, w_gate_up, b_gate_up, w_down, b_down):
    B, S, D = x_prompt.shape
    Bs, Ss, _ = x_sample.shape
    depth = w_in.shape[0]
    H = a_log.shape[1]
    W = H * HEAD_DIM
    n_exp = w_router.shape[2]
    page = cache_k.shape[2]
    n_pages = page_table.shape[1]
    past = n_pages * page
    RS = SAMPLE_ROWS
    assert Ss == 1 and Bs <= 8 and S % MOBA_BLOCK == 0 and S >= CONV_W - 1
    assert past % MOBA_BLOCK == 0 and past // MOBA_BLOCK >= MOBA_TOPK and MOBA_BLOCK % page == 0
    assert n_exp <= LANES and 2 * H <= LANES and H % 2 == 0 and cache_k.shape[3] == H and w_in.shape[2] == 7 * W + 2 * H
    alpha = (2 * depth) ** 0.25
    MP = B * S

    w_main = jnp.concatenate([w_in[:, :, :4 * W], w_in[:, :, 4 * W + 2 * H:]], axis=-1).astype(BF16)
    w_small = jnp.pad(w_in[:, :, 4 * W:4 * W + 2 * H], ((0, 0), (0, 0), (0, LANES - 2 * H)))
    w_o_bf = w_o.astype(BF16)
    w_router_p = jnp.pad(w_router, ((0, 0), (0, 0), (0, LANES - n_exp)))
    b_router_p = jnp.pad(b_router, ((0, 0), (0, LANES - n_exp))).reshape(depth, 1, LANES)
    ln1_g3, ln1_b3 = ln1_g.reshape(depth, 1, D), ln1_b.reshape(depth, 1, D)
    ln2_g3, ln2_b3 = ln2_g.reshape(depth, 1, D), ln2_b.reshape(depth, 1, D)
    b_gu4 = b_gate_up.reshape(depth, n_exp, 1, -1)
    b_dn4 = b_down.reshape(depth, n_exp, 1, D)
    cos_p, sin_p = _rope_tables(jnp.arange(S, dtype=I32))
    cos_s, sin_s = _rope_tables(jnp.full((1,), past, I32))

    c_all = jnp.concatenate([c_prompt, jnp.zeros((8 - B % 8 if B % 8 else 0, D), F32),
                             c_sample, jnp.zeros((RS - Bs, D), F32)], axis=0)
    s_row0 = c_all.shape[0] - RS
    mod_all = _ada_all(c_all, w_ada, b_ada)

    xp = x_prompt.reshape(MP, D)
    xs = jnp.pad(x_sample.reshape(Bs, D), ((0, RS - Bs), (0, 0)))
    zero_conv = jnp.zeros((B, 8, 3 * W), F32)
    zero_state = jnp.zeros((B, H, HEAD_DIM, HEAD_DIM), F32)
    outs = [[] for _ in range(8)]
    ppb = MOBA_BLOCK // page

    for l in range(depth):
        mod_p = mod_all[l, :B].reshape(B, 1, 6 * D)
        mod_s = mod_all[l, s_row0:].reshape(1, RS, 6 * D)
        alog_row = _lane_row(a_log[l], H)
        dtb_row = _lane_row(dt_bias[l], H)
        nw = gdn_norm_w[l].reshape(1, HEAD_DIM)

        proj_p, small_p = _inproj(xp, mod_p, w_main, w_small, l, tm=_tile(S, 1024, 8))
        oa_p, gdn_p = _gdn(proj_p, small_p, zero_conv, zero_state, conv_w[l], alog_row, dtb_row, nw,
                           H, GDN_CHUNK, False)
        q_rot, k_rot, kmean = _rope_prompt(proj_p, cos_p, sin_p, H, B)
        ob_p = _moba_prompt(q_rot, k_rot, proj_p, kmean, H, B)

        proj_s, small_s = _inproj(xs, mod_s, w_main, w_small, l, tm=RS)
        pad_rows = lambda t: jnp.pad(t[:Bs, None, :], ((0, 0), (0, GDN_CHUNK - 1), (0, 0))).reshape(Bs * GDN_CHUNK, -1)
        conv_prev8 = jnp.pad(state_conv[l], ((0, 0), (8 - (CONV_W - 1), 0), (0, 0)))
        oa_s64, gdn_s = _gdn(pad_rows(proj_s), pad_rows(small_s), conv_prev8, state_gdn[l], conv_w[l],
                             alog_row, dtb_row, nw, H, 1, True)
        oa_s = jnp.pad(oa_s64[::GDN_CHUNK], ((0, RS - Bs), (0, 0)))
        qs_rot, ks_rot = _rope_sample(proj_s, cos_s, sin_s, H)
        q3 = qs_rot[:Bs].reshape(Bs, H, HEAD_DIM)
        kn3 = ks_rot[:Bs].reshape(Bs, H, HEAD_DIM)
        vn3 = proj_s[:Bs, 6 * W:7 * W].reshape(Bs, H, HEAD_DIM)
        kmean_s = _pool_block_means(cache_k, page_table, l)
        sel = _sample_select(q3, kmean_s)[:, :, :MOBA_TOPK]
        lpage = sel[..., None] * ppb + jnp.arange(ppb, dtype=I32)
        pages = jnp.take_along_axis(page_table, lpage.reshape(Bs, -1), axis=1)
        ob_s3 = _sample_attn(pages.reshape(-1).astype(I32), q3, cache_k, cache_v, kn3, vn3, l, MOBA_TOPK * ppb)
        ob_s = jnp.pad(ob_s3.reshape(Bs, W), ((0, RS - Bs), (0, 0))).astype(BF16)

        x1_p, h2_p, lg_p = _oproj(oa_p, ob_p, w_o_bf, xp, mod_p, ln1_g3, ln1_b3, w_router_p, b_router_p, l, 256, alpha)
        x1_s, h2_s, lg_s = _oproj(oa_s, ob_s, w_o_bf, xs, mod_s, ln1_g3, ln1_b3, w_router_p, b_router_p, l, RS, alpha)

        lg_all = jnp.concatenate([lg_p, lg_s], axis=0)
        m_all = MP + RS
        top_e, gates = _route(lg_all, n_exp, _tile(m_all, 1024, 8))
        dest, slot_tok, ge, rows = _dispatch(top_e[:, :TOPK_E], n_exp)
        h2_all = jnp.concatenate([h2_p, h2_s, jnp.zeros((8, D), F32)], axis=0)
        y = _moe_experts(ge, rows, slot_tok, h2_all, w_gate_up, b_gu4, w_down, b_dn4, l)

        tf = _tile(S, 256, 8)
        xp = _final(dest[:MP * TOPK_E].reshape(MP // tf, 1, TOPK_E * tf), y, x1_p, gates[:MP], mod_p,
                    ln2_g3, ln2_b3, l, tf, alpha)
        xs = _final(dest[MP * TOPK_E:].reshape(1, 1, TOPK_E * RS), y, x1_s, gates[MP:], mod_s,
                    ln2_g3, ln2_b3, l, RS, alpha)

        outs[0].append(k_rot.reshape(B, S, H, HEAD_DIM))
        outs[1].append(proj_p[:, 6 * W:7 * W].reshape(B, S, H, HEAD_DIM))
        outs[2].append(kn3.reshape(Bs, 1, H, HEAD_DIM))
        outs[3].append(vn3.reshape(Bs, 1, H, HEAD_DIM))
        outs[4].append(gdn_p)
        outs[5].append(gdn_s)
        outs[6].append(proj_p.reshape(B, S, -1)[:, S - (CONV_W - 1):, :3 * W])
        outs[7].append(jnp.concatenate([state_conv[l][:, 1:], proj_s[:Bs, None, :3 * W]], axis=1))

    return (xp.reshape(B, S, D), xs[:Bs].reshape(Bs, 1, D)) + tuple(jnp.stack(o) for o in outs)
```
